```python
import math
import numpy as np
import jax
import jax.numpy as jnp
from jax import lax

D_MODEL = 4096
BATCH = 8
SEQ = 2048
DEPTH = 2

CTX_LEN = 256
GRID_W = 64
HEAD_DIM = 128
MIX_W = D_MODEL
N_MIX_HEADS = MIX_W // HEAD_DIM
GDN_HEADS = 3 * N_MIX_HEADS // 8
ATT_HEADS = 3 * N_MIX_HEADS // 8
ATT_KV_HEADS = ATT_HEADS // 3
HGRN_HEADS = N_MIX_HEADS - GDN_HEADS - ATT_HEADS
GDN_W = GDN_HEADS * HEAD_DIM
ATT_W = ATT_HEADS * HEAD_DIM
ATT_KV_W = ATT_KV_HEADS * HEAD_DIM
HGRN_W = HGRN_HEADS * HEAD_DIM
CONV_W = 5
CHUNK = 64
Q_BLOCK = 128
ROPE_THETA = 10000.0
N_GROUPS = 4
EXPERTS_PER_GROUP = 8
N_EXPERTS = N_GROUPS * EXPERTS_PER_GROUP
TOP_K = 2
EXPERT_FF = 512
ADA_CHUNKS = 6
EPS = 1e-6

IN_SIZES = (3 * GDN_W, GDN_W, 2 * GDN_HEADS, 2 * GDN_HEADS, ATT_W, ATT_KV_W, ATT_KV_W,
            HGRN_W, HGRN_W, HGRN_W, HGRN_W, HGRN_W)
IN_OFFSETS = tuple(int(o) for o in np.cumsum(IN_SIZES)[:-1])
N_IN = sum(IN_SIZES)

kernel_name = 'hybrid_gdn_gqa_hgrn2_hmoe_prefix_dit'


def rmsnorm(x, g):
    xf = x.astype(jnp.float32)
    y = xf * lax.rsqrt(jnp.mean(xf * xf, axis=-1, keepdims=True) + EPS)
    return (y * g.astype(jnp.float32)).astype(x.dtype)


def l2norm(x):
    return x * lax.rsqrt(jnp.sum(x * x, axis=-1, keepdims=True) + EPS)


def short_conv(u, w):
    pad = CONV_W // 2
    return lax.conv_general_dilated(u, w[:, None, :].astype(u.dtype), window_strides=(1,),
                                    padding=((pad, pad),), dimension_numbers=('NWC', 'WIO', 'NWC'),
                                    feature_group_count=u.shape[-1])


def _split_heads(t, n_heads):
    return t.reshape(t.shape[:2] + (n_heads, HEAD_DIM))


def _chunk(t):
    b, l, h = t.shape[:3]
    t = t.reshape((b, l // CHUNK, CHUNK, h) + t.shape[3:])
    return jnp.moveaxis(t, (1, 2), (0, 3))


def _unchunk(t):
    t = jnp.moveaxis(t, (0, 3), (1, 2))
    return t.reshape((t.shape[0], t.shape[1] * t.shape[2]) + t.shape[3:])


def _rotate_half_axial(x):
    def rh(v):
        a, b = jnp.split(v, 2, axis=-1)
        return jnp.concatenate([-b, a], axis=-1)
    xr, xc = jnp.split(x, 2, axis=-1)
    return jnp.concatenate([rh(xr), rh(xc)], axis=-1)


def axial_rope_tables(n_tokens):
    rows = n_tokens // GRID_W
    row = jnp.repeat(jnp.arange(rows, dtype=jnp.float32), GRID_W)
    col = jnp.tile(jnp.arange(GRID_W, dtype=jnp.float32), rows)
    half = HEAD_DIM // 2
    inv_freq = ROPE_THETA ** (-jnp.arange(0, half, 2, dtype=jnp.float32) / half)
    ang_r = row[:, None] * inv_freq[None, :]
    ang_c = col[:, None] * inv_freq[None, :]
    ang = jnp.concatenate([ang_r, ang_r, ang_c, ang_c], axis=-1)
    return jnp.cos(ang)[:, None, :], jnp.sin(ang)[:, None, :]


def apply_rope(x, cos, sin):
    xf = x.astype(jnp.float32)
    return (xf * cos + _rotate_half_axial(xf) * sin).astype(x.dtype)


def _bidir(scan_fn, ctx_f, lat_f, ctx_b, lat_b, s0):
    flip = lambda ts: tuple(jnp.flip(t, axis=1) for t in ts)
    o_cf, s_cf = scan_fn(*ctx_f, s0)
    o_lf, _ = scan_fn(*lat_f, s_cf)
    o_cb, s_cb = scan_fn(*flip(ctx_b), s0)
    o_lb, _ = scan_fn(*flip(lat_b), s_cb)
    return o_cf + jnp.flip(o_cb, axis=1), o_lf + jnp.flip(o_lb, axis=1)


def gated_delta_scan(q, k, v, g, beta, s0):
    dv = v.shape[-1]
    qc, kc, vc = _chunk(q), _chunk(k), _chunk(v)
    gc = jnp.cumsum(_chunk(g), axis=-1)
    bc = _chunk(beta)
    idx = jnp.arange(CHUNK)
    incl = idx[:, None] >= idx[None, :]
    strict = idx[:, None] > idx[None, :]
    decay_incl = jnp.exp(jnp.where(incl, gc[..., :, None] - gc[..., None, :], -jnp.inf))
    decay_strict = jnp.where(strict, decay_incl, 0.0)
    kb = kc * bc[..., None]
    a_kk = jnp.einsum('nbhid,nbhjd->nbhij', kb, kc) * decay_strict
    eye = jnp.eye(CHUNK, dtype=a_kk.dtype)
    rhs = jnp.concatenate([vc * bc[..., None], kb * jnp.exp(gc)[..., None]], axis=-1)
    sol = lax.linalg.triangular_solve(eye + a_kk, rhs, left_side=True, lower=True)
    u, w = sol[..., :dv], sol[..., dv:]
    a_qk = jnp.einsum('nbhid,nbhjd->nbhij', qc, kc) * decay_incl

    def step(s, inp):
        q_i, k_i, u_i, w_i, a_i, g_i = inp
        v_new = u_i - jnp.einsum('bhck,bhkv->bhcv', w_i, s)
        o = (jnp.einsum('bhck,bhkv->bhcv', q_i * jnp.exp(g_i)[..., None], s)
             + jnp.einsum('bhij,bhjv->bhiv', a_i, v_new))
        g_last = g_i[..., -1:]
        s = (jnp.exp(g_last)[..., None] * s
             + jnp.einsum('bhck,bhcv->bhkv', k_i * jnp.exp(g_last - g_i)[..., None], v_new))
        return s, o

    s_fin, o = lax.scan(step, s0, (qc, kc, u, w, a_qk, gc))
    return _unchunk(o), s_fin


def gla_scan(q, k, v, log_f, s0):
    qc, kc, vc = _chunk(q), _chunk(k), _chunk(v)
    gc = jnp.cumsum(_chunk(log_f), axis=-2)
    idx = jnp.arange(CHUNK)
    incl = (idx[:, None] >= idx[None, :])[:, :, None]

    def step(s, inp):
        q_i, k_i, v_i, g_i = inp
        dec = jnp.exp(jnp.where(incl, g_i[:, :, :, None, :] - g_i[:, :, None, :, :], -jnp.inf))
        a = jnp.einsum('bhid,bhjd,bhijd->bhij', q_i, k_i, dec)
        o = (jnp.einsum('bhck,bhkv->bhcv', q_i * jnp.exp(g_i), s)
             + jnp.einsum('bhij,bhjv->bhiv', a, v_i))
        g_last = g_i[:, :, -1:, :]
        s = (jnp.exp(g_last[:, :, 0, :])[..., None] * s
             + jnp.einsum('bhck,bhcv->bhkv', k_i * jnp.exp(g_last - g_i), v_i))
        return s, o

    s_fin, o = lax.scan(step, s0, (qc, kc, vc, gc))
    return _unchunk(o), s_fin


def _gated_head_norm(o, z, g):
    bsz, n = z.shape[:2]
    zh = z.reshape(bsz, n, -1, HEAD_DIM).astype(jnp.float32)
    return (rmsnorm(o, g) * jax.nn.silu(zh)).reshape(bsz, n, -1).astype(z.dtype)


def _gdn_prep(qkv, a, b, conv_w, a_log, dt_bias):
    qkv = jax.nn.silu(short_conv(qkv, conv_w)).astype(jnp.float32)
    q, k, v = jnp.split(qkv, 3, axis=-1)
    q = l2norm(_split_heads(q, GDN_HEADS)) * HEAD_DIM ** -0.5
    k = l2norm(_split_heads(k, GDN_HEADS))
    v = _split_heads(v, GDN_HEADS)
    bsz, n = a.shape[:2]
    a = a.astype(jnp.float32).reshape(bsz, n, 2, GDN_HEADS)
    beta = jax.nn.sigmoid(b.astype(jnp.float32).reshape(bsz, n, 2, GDN_HEADS))
    g = -jnp.exp(a_log.astype(jnp.float32)) * jax.nn.softplus(a + dt_bias.astype(jnp.float32))
    return (q, k, v, g[:, :, 0], beta[:, :, 0]), (q, k, v, g[:, :, 1], beta[:, :, 1])


def gdn_mixer(ctx_in, lat_in, conv_w, a_log, dt_bias, norm_g, need_ctx):
    ctx_f, ctx_b = _gdn_prep(ctx_in[0], ctx_in[2], ctx_in[3], conv_w, a_log, dt_bias)
    lat_f, lat_b = _gdn_prep(lat_in[0], lat_in[2], lat_in[3], conv_w, a_log, dt_bias)
    s0 = jnp.zeros((ctx_in[0].shape[0], GDN_HEADS, HEAD_DIM, HEAD_DIM), jnp.float32)
    o_c, o_l = _bidir(gated_delta_scan, ctx_f, lat_f, ctx_b, lat_b, s0)
    y_l = _gated_head_norm(o_l, lat_in[1], norm_g)
    y_c = _gated_head_norm(o_c, ctx_in[1], norm_g) if need_ctx else None
    return y_c, y_l


def _gqa_attend(q, k, v):
    bsz, lq = q.shape[:2]
    qg = q.reshape(bsz, lq, ATT_KV_HEADS, ATT_HEADS // ATT_KV_HEADS, HEAD_DIM)
    s = jnp.einsum('bqkgd,bskd->bkgqs', qg, k, preferred_element_type=jnp.float32) * (HEAD_DIM ** -0.5)
    p = jax.nn.softmax(s, axis=-1).astype(v.dtype)
    o = jnp.einsum('bkgqs,bskd->bqkgd', p, v)
    return o.reshape(bsz, lq, ATT_W)


def attention_mixer(ctx_in, lat_in, q_norm_g, k_norm_g, cos, sin, need_ctx):
    q_c, k_c, v_c = ctx_in
    q_l, k_l, v_l = lat_in
    k_c = rmsnorm(_split_heads(k_c, ATT_KV_HEADS), k_norm_g)
    k_l = apply_rope(rmsnorm(_split_heads(k_l, ATT_KV_HEADS), k_norm_g), cos, sin)
    q_l = apply_rope(rmsnorm(_split_heads(q_l, ATT_HEADS), q_norm_g), cos, sin)
    v_c = _split_heads(v_c, ATT_KV_HEADS)
    v_l = _split_heads(v_l, ATT_KV_HEADS)
    k_all = jnp.concatenate([k_c, k_l], axis=1)
    v_all = jnp.concatenate([v_c, v_l], axis=1)
    bsz, n_lat = q_l.shape[:2]
    q_blocks = jnp.moveaxis(q_l.reshape(bsz, n_lat // Q_BLOCK, Q_BLOCK, ATT_HEADS, HEAD_DIM), 1, 0)
    y_l = lax.map(lambda qb: _gqa_attend(qb, k_all, v_all), q_blocks)
    y_l = jnp.moveaxis(y_l, 0, 1).reshape(bsz, n_lat, ATT_W)
    y_c = _gqa_attend(rmsnorm(_split_heads(q_c, ATT_HEADS), q_norm_g), k_c, v_c) if need_ctx else None
    return y_c, y_l


def _hgrn_prep(q, f_fwd, f_bwd, i, lb):
    qh = jax.nn.silu(_split_heads(q, HGRN_HEADS).astype(jnp.float32)) * HEAD_DIM ** -0.5
    vh = _split_heads(i, HGRN_HEADS).astype(jnp.float32)
    lbh = lb.astype(jnp.float32).reshape(HGRN_HEADS, HEAD_DIM)

    def gate(f):
        z = _split_heads(f, HGRN_HEADS).astype(jnp.float32)
        log_f = jnp.logaddexp(jax.nn.log_sigmoid(z), jnp.log(lbh) + jax.nn.log_sigmoid(-z))
        return (1.0 - lbh) * jax.nn.sigmoid(-z), log_f

    k_f, lf_f = gate(f_fwd)
    k_b, lf_b = gate(f_bwd)
    return (qh, k_f, vh, lf_f), (qh, k_b, vh, lf_b)


def hgrn_mixer(ctx_in, lat_in, lb, norm_g, need_ctx):
    ctx_f, ctx_b = _hgrn_prep(ctx_in[0], ctx_in[1], ctx_in[2], ctx_in[3], lb)
    lat_f, lat_b = _hgrn_prep(lat_in[0], lat_in[1], lat_in[2], lat_in[3], lb)
    s0 = jnp.zeros((ctx_in[0].shape[0], HGRN_HEADS, HEAD_DIM, HEAD_DIM), jnp.float32)
    o_c, o_l = _bidir(gla_scan, ctx_f, lat_f, ctx_b, lat_b, s0)
    y_l = _gated_head_norm(o_l, lat_in[4], norm_g)
    y_c = _gated_head_norm(o_c, ctx_in[4], norm_g) if need_ctx else None
    return y_c, y_l


def hier_moe(h, rgw, rgb, rew, reb, w_gate, w_up, w_down):
    shp = h.shape
    t = h.reshape(-1, shp[-1])
    n_tok = t.shape[0]
    grp_logits = (t @ rgw + rgb).astype(jnp.float32)
    p_grp = jax.nn.softmax(grp_logits, axis=-1)
    grp = jnp.argmax(grp_logits, axis=-1)
    exp_logits = (t @ rew + reb).astype(jnp.float32).reshape(n_tok, N_GROUPS, EXPERTS_PER_GROUP)
    tok = jnp.arange(n_tok)
    p_in = jax.nn.softmax(exp_logits[tok, grp], axis=-1)
    top_p, top_i = lax.top_k(p_in, TOP_K)
    w = p_grp[tok, grp][:, None] * top_p / jnp.sum(top_p, axis=-1, keepdims=True)
    eid = grp[:, None] * EXPERTS_PER_GROUP + top_i
    combine = jnp.sum(jax.nn.one_hot(eid, N_EXPERTS, dtype=jnp.float32) * w[..., None], axis=1).astype(t.dtype)
    y = jnp.zeros_like(t)
    for e in range(N_EXPERTS):
        he = jax.nn.silu(t @ w_gate[e]) * (t @ w_up[e])
        y = y + combine[:, e:e + 1] * (he @ w_down[e])
    return y.reshape(shp)


def _layer(x_lat, x_ctx, silu_c, silu_cc, cos, sin, lb, p, need_ctx):
    mod_l = jnp.split((silu_c @ p['ada_w'] + p['ada_b'])[:, None, :], ADA_CHUNKS, axis=-1)
    mod_c = jnp.split(silu_cc @ p['ada_w'] + p['ada_b'], ADA_CHUNKS, axis=-1)
    n_ctx = x_ctx.shape[1]

    h_c = rmsnorm(x_ctx, p['norm1_g']) * (1 + mod_c[1]) + mod_c[0]
    h_l = rmsnorm(x_lat, p['norm1_g']) * (1 + mod_l[1]) + mod_l[0]
    proj = jnp.concatenate([h_c, h_l], axis=1) @ p['w_in']
    groups = jnp.split(proj, IN_OFFSETS, axis=-1)
    cg = [gr[:, :n_ctx] for gr in groups]
    lg = [gr[:, n_ctx:] for gr in groups]
    gdn_c, gdn_l = gdn_mixer(cg[0:4], lg[0:4], p['gdn_conv_w'], p['gdn_a_log'], p['gdn_dt_bias'],
                             p['gdn_norm_g'], need_ctx)
    att_c, att_l = attention_mixer(cg[4:7], lg[4:7], p['attn_q_norm_g'], p['attn_k_norm_g'], cos, sin, need_ctx)
    hg_c, hg_l = hgrn_mixer(cg[7:12], lg[7:12], lb, p['hgrn_norm_g'], need_ctx)
    x_lat = x_lat + mod_l[2] * (jnp.concatenate([gdn_l, att_l, hg_l], axis=-1) @ p['w_out'])

    moe_args = (p['router_group_w'], p['router_group_b'], p['router_expert_w'], p['router_expert_b'],
                p['moe_w_gate'], p['moe_w_up'], p['moe_w_down'])
    h2_l = rmsnorm(x_lat, p['norm2_g']) * (1 + mod_l[4]) + mod_l[3]
    if need_ctx:
        x_ctx = x_ctx + mod_c[2] * (jnp.concatenate([gdn_c, att_c, hg_c], axis=-1) @ p['w_out'])
        h2_c = rmsnorm(x_ctx, p['norm2_g']) * (1 + mod_c[4]) + mod_c[3]
        y = hier_moe(jnp.concatenate([h2_c, h2_l], axis=1), *moe_args)
        x_ctx = x_ctx + mod_c[5] * y[:, :n_ctx]
        x_lat = x_lat + mod_l[5] * y[:, n_ctx:]
    else:
        x_lat = x_lat + mod_l[5] * hier_moe(h2_l, *moe_args)
    return x_lat, x_ctx


def setup_inputs(seed: int = 0) -> dict:
    key = jax.random.key(seed)
    ks = jax.random.split(key, 27)
    nrm = jax.random.normal
    d = D_MODEL
    f32 = jnp.float32

    def gain(k, shape):
        return 1.0 + 0.02 * nrm(k, shape, f32)

    dt = jnp.exp(jax.random.uniform(ks[9], (DEPTH, 2, GDN_HEADS), f32, math.log(1e-3), math.log(1e-1)))
    return {
        'x': nrm(ks[0], (BATCH, SEQ, d), f32),
        'c': nrm(ks[1], (BATCH, d), f32),
        'ctx': nrm(ks[2], (BATCH, CTX_LEN, d), f32),
        'c_ctx': nrm(ks[3], (d,), f32),
        'ada_w': nrm(ks[4], (DEPTH, d, ADA_CHUNKS * d), f32) * (0.5 * d ** -0.5),
        'ada_b': 0.02 * nrm(ks[5], (DEPTH, ADA_CHUNKS * d), f32),
        'norm1_g': gain(ks[6], (DEPTH, d)),
        'norm2_g': gain(ks[7], (DEPTH, d)),
        'w_in': nrm(ks[8], (DEPTH, d, N_IN), f32) * d ** -0.5,
        'gdn_conv_w': nrm(ks[10], (DEPTH, CONV_W, 3 * GDN_W), f32) * CONV_W ** -0.5,
        'gdn_a_log': jnp.log(jax.random.uniform(ks[11], (DEPTH, 2, GDN_HEADS), f32, 1.0, 16.0)),
        'gdn_dt_bias': dt + jnp.log(-jnp.expm1(-dt)),
        'gdn_norm_g': gain(ks[12], (DEPTH, HEAD_DIM)),
        'attn_q_norm_g': gain(ks[13], (DEPTH, HEAD_DIM)),
        'attn_k_norm_g': gain(ks[14], (DEPTH, HEAD_DIM)),
        'hgrn_lb_logits': 0.5 * nrm(ks[15], (DEPTH, HGRN_W), f32),
        'hgrn_norm_g': gain(ks[16], (DEPTH, HEAD_DIM)),
        'w_out': nrm(ks[17], (DEPTH, MIX_W, d), f32) * MIX_W ** -0.5,
        'router_group_w': nrm(ks[18], (DEPTH, d, N_GROUPS), f32) * d ** -0.5,
        'router_group_b': 0.01 * nrm(ks[19], (DEPTH, N_GROUPS), f32),
        'router_expert_w': nrm(ks[20], (DEPTH, d, N_EXPERTS), f32) * d ** -0.5,
        'router_expert_b': 0.01 * nrm(ks[21], (DEPTH, N_EXPERTS), f32),
        'moe_w_gate': nrm(ks[22], (DEPTH, N_EXPERTS, d, EXPERT_FF), f32) * d ** -0.5,
        'moe_w_up': nrm(ks[23], (DEPTH, N_EXPERTS, d, EXPERT_FF), f32) * d ** -0.5,
        'moe_w_down': nrm(ks[24], (DEPTH, N_EXPERTS, EXPERT_FF, d), f32) * EXPERT_FF ** -0.5,
        'final_norm_g': gain(ks[26], (d,)),
    }


def reference(x, c, ctx, c_ctx, ada_w, ada_b, norm1_g, norm2_g, w_in, gdn_conv_w, gdn_a_log, gdn_dt_bias,
              gdn_norm_g, attn_q_norm_g, attn_k_norm_g, hgrn_lb_logits, hgrn_norm_g, w_out,
              router_group_w, router_group_b, router_expert_w, router_expert_b,
              moe_w_gate, moe_w_up, moe_w_down, final_norm_g):
    cos, sin = axial_rope_tables(x.shape[1])
    lb_p = jax.nn.softmax(hgrn_lb_logits.astype(jnp.float32), axis=0)
    lb_cum = jnp.cumsum(lb_p, axis=0)
    lb_all = lb_cum - lb_cum[0:1]
    silu_c = jax.nn.silu(c)
    silu_cc = jax.nn.silu(c_ctx)
    x_lat, x_ctx = x, ctx
    for l in range(DEPTH):
        p = {
            'ada_w': ada_w[l], 'ada_b': ada_b[l], 'norm1_g': norm1_g[l], 'norm2_g': norm2_g[l],
            'w_in': w_in[l], 'gdn_conv_w': gdn_conv_w[l], 'gdn_a_log': gdn_a_log[l],
            'gdn_dt_bias': gdn_dt_bias[l], 'gdn_norm_g': gdn_norm_g[l],
            'attn_q_norm_g': attn_q_norm_g[l], 'attn_k_norm_g': attn_k_norm_g[l],
            'hgrn_norm_g': hgrn_norm_g[l], 'w_out': w_out[l],
            'router_group_w': router_group_w[l], 'router_group_b': router_group_b[l],
            'router_expert_w': router_expert_w[l], 'router_expert_b': router_expert_b[l],
            'moe_w_gate': moe_w_gate[l], 'moe_w_up': moe_w_up[l], 'moe_w_down': moe_w_down[l],
        }
        x_lat, x_ctx = _layer(x_lat, x_ctx, silu_c, silu_cc, cos, sin, lb_all[l], p,
                              need_ctx=(l < DEPTH - 1))
    return rmsnorm(x_lat, final_norm_g)
```

```python
import functools
import math

import jax
import jax.numpy as jnp
from jax import lax
from jax.experimental import pallas as pl
from jax.experimental.pallas import tpu as pltpu

HEAD_DIM = 128
CHUNK = 64
SUB = 16
GRID_W = 64
ROPE_THETA = 10000.0
N_GROUPS = 4
EXPERTS_PER_GROUP = 8
ADA_CHUNKS = 6
EPS = 1e-6
LANES = 128
MOE_TILE = 256
VMEM_LIMIT = 56 * 1024 * 1024

F32 = jnp.float32
BF16 = jnp.bfloat16
NEG_INF = float("-inf")


def _cp(sem, vmem=VMEM_LIMIT):
    return pltpu.CompilerParams(dimension_semantics=sem, vmem_limit_bytes=vmem)


def _round_up(x, m):
    return (x + m - 1) // m * m


def _sigmoid(x):
    return 1.0 / (1.0 + jnp.exp(-x))


def _silu(x):
    return x * _sigmoid(x)


def _log_sigmoid(x):
    return jnp.minimum(x, 0.0) - jnp.log1p(jnp.exp(-jnp.abs(x)))


def _dot(a, b):
    return jnp.dot(a, b, preferred_element_type=F32)


def _dot_nt(a, b):
    return lax.dot_general(a, b, (((1,), (1,)), ((), ())), preferred_element_type=F32)


def _ada_kernel(c_ref, w_ref, b_ref, o_ref):
    c = c_ref[...]
    s = _silu(c).astype(BF16)
    o_ref[...] = _dot(s, w_ref[...].astype(BF16)) + b_ref[...]


def _ada_modulation(cc, ada_w, ada_b):
    depth, d, n = ada_w.shape
    rows = cc.shape[0]
    tn = 512 if n % 512 == 0 else LANES
    return pl.pallas_call(
        _ada_kernel,
        out_shape=jax.ShapeDtypeStruct((depth, rows, n), F32),
        grid=(depth, n // tn),
        in_specs=[
            pl.BlockSpec((rows, d), lambda l, j: (0, 0)),
            pl.BlockSpec((None, d, tn), lambda l, j: (l, 0, j)),
            pl.BlockSpec((None, 1, tn), lambda l, j: (l, 0, j)),
        ],
        out_specs=pl.BlockSpec((None, rows, tn), lambda l, j: (l, 0, j)),
        compiler_params=_cp(("parallel", "parallel")),
        name="ada_mod",
    )(cc, ada_w, ada_b.reshape(depth, 1, n))


def _norm_mod_kernel(x_ref, g_ref, sh_ref, sc_ref, o_ref):
    x = x_ref[...]
    y = x * lax.rsqrt(jnp.mean(x * x, axis=-1, keepdims=True) + EPS) * g_ref[...]
    o_ref[...] = (y * (1.0 + sc_ref[...]) + sh_ref[...]).astype(o_ref.dtype)


def _mod_row(b, j, ncb, ctx_row):
    return jnp.where(j < ncb, ctx_row, b)


def _norm_mod(x, g, mod4, shift_chunk, scale_chunk, ts, ncb):
    bsz, s, d = x.shape
    ctx_row = bsz
    return pl.pallas_call(
        _norm_mod_kernel,
        out_shape=jax.ShapeDtypeStruct((bsz, s, d), BF16),
        grid=(bsz, s // ts),
        in_specs=[
            pl.BlockSpec((None, ts, d), lambda b, j: (b, j, 0)),
            pl.BlockSpec((1, d), lambda b, j: (0, 0)),
            pl.BlockSpec((None, 1, d), lambda b, j: (_mod_row(b, j, ncb, ctx_row), 0, shift_chunk)),
            pl.BlockSpec((None, 1, d), lambda b, j: (_mod_row(b, j, ncb, ctx_row), 0, scale_chunk)),
        ],
        out_specs=pl.BlockSpec((None, ts, d), lambda b, j: (b, j, 0)),
        compiler_params=_cp(("parallel", "parallel")),
        name="norm_mod",
    )(x, g.reshape(1, d), mod4, mod4)


def _mm_kernel(a_ref, w_ref, o_ref):
    o_ref[...] = _dot(a_ref[...], w_ref[...]).astype(o_ref.dtype)


def _matmul(a, w, tm, tn):
    m, k = a.shape
    n = w.shape[1]
    return pl.pallas_call(
        _mm_kernel,
        out_shape=jax.ShapeDtypeStruct((m, n), F32),
        grid=(n // tn, m // tm),
        in_specs=[
            pl.BlockSpec((tm, k), lambda j, i: (i, 0)),
            pl.BlockSpec((k, tn), lambda j, i: (0, j)),
        ],
        out_specs=pl.BlockSpec((tm, tn), lambda j, i: (i, j)),
        compiler_params=_cp(("parallel", "parallel")),
        name="in_proj",
    )(a, w)


def _mm_res_kernel(a_ref, w_ref, x_ref, gate_ref, o_ref):
    o_ref[...] = x_ref[...] + gate_ref[...] * _dot(a_ref[...], w_ref[...])


def _out_proj(y, w, x, mod4, gate_chunk, ts, ncb, row_tile0):
    bsz, s, d = x.shape
    k = y.shape[-1]
    tn = 2048 if d % 2048 == 0 else d
    ctx_row = bsz
    nt = s // ts - row_tile0
    gpc = d // tn
    return pl.pallas_call(
        _mm_res_kernel,
        out_shape=jax.ShapeDtypeStruct((bsz, s, d), F32),
        grid=(gpc, bsz, nt),
        in_specs=[
            pl.BlockSpec((None, ts, k), lambda n, b, i: (b, i + row_tile0, 0)),
            pl.BlockSpec((k, tn), lambda n, b, i: (0, n)),
            pl.BlockSpec((None, ts, tn), lambda n, b, i: (b, i + row_tile0, n)),
            pl.BlockSpec((None, 1, tn),
                         lambda n, b, i: (_mod_row(b, i + row_tile0, ncb, ctx_row), 0, gate_chunk * gpc + n)),
        ],
        out_specs=pl.BlockSpec((None, ts, tn), lambda n, b, i: (b, i + row_tile0, n)),
        compiler_params=_cp(("parallel", "parallel", "parallel")),
        name="out_proj",
    )(y, w, x, mod4)


def _gdn_prep_kernel(x_ref, w_ref, o_ref, *, n_ctx, gdn_w, conv_w):
    j = pl.program_id(1)
    x = x_ref[...]
    s, tc = x.shape
    pad = conv_w // 2
    row = lax.broadcasted_iota(jnp.int32, (s, 1), 0)
    is_lat = row >= n_ctx
    acc = jnp.zeros_like(x)
    for tap in range(conv_w):
        sft = pad - tap
        xs = x if sft == 0 else pltpu.roll(x, sft % s, 0)
        src = row - sft
        valid = (src >= 0) & (src < s) & ((src >= n_ctx) == is_lat)
        acc = acc + w_ref[tap:tap + 1, :] * jnp.where(valid, xs, 0.0)
    y = _silu(acc)
    kind = (j * tc) // gdn_w
    scale = jnp.where(kind == 0, HEAD_DIM ** -0.5, 1.0).astype(F32)
    normed = kind < 2
    for h in range(tc // HEAD_DIM):
        sl = slice(h * HEAD_DIM, (h + 1) * HEAD_DIM)
        yh = y[:, sl]
        nh = yh * lax.rsqrt(jnp.sum(yh * yh, axis=-1, keepdims=True) + EPS) * scale
        o_ref[:, sl] = jnp.where(normed, nh, yh)


def _gdn_prep(proj, conv_w_arr, lay, n_ctx, hb):
    bsz, s, _ = proj.shape
    gdn_w = lay["gdn_w"]
    tc = hb * HEAD_DIM
    kw = conv_w_arr.shape[0]
    return pl.pallas_call(
        functools.partial(_gdn_prep_kernel, n_ctx=n_ctx, gdn_w=gdn_w, conv_w=kw),
        out_shape=jax.ShapeDtypeStruct((bsz, s, 3 * gdn_w), F32),
        grid=(bsz, 3 * gdn_w // tc),
        in_specs=[
            pl.BlockSpec((None, s, tc), lambda b, j: (b, 0, j)),
            pl.BlockSpec((kw, tc), lambda b, j: (0, j)),
        ],
        out_specs=pl.BlockSpec((None, s, tc), lambda b, j: (b, 0, j)),
        compiler_params=_cp(("parallel", "parallel")),
        name="gdn_prep",
    )(proj, conv_w_arr)


def _cumsum_rows(x, rev):
    c = x.shape[0]
    row = lax.broadcasted_iota(jnp.int32, (c, 1), 0)
    step = 1
    while step < c:
        if rev:
            x = x + jnp.where(row < c - step, pltpu.roll(x, c - step, 0), 0.0)
        else:
            x = x + jnp.where(row >= step, pltpu.roll(x, step, 0), 0.0)
        step *= 2
    return x


def _gdn_chunk(q, k, v, gc_col, gc_row, beta_col, state, rev):
    c = q.shape[0]
    ii = lax.broadcasted_iota(jnp.int32, (c, c), 0)
    jj = lax.broadcasted_iota(jnp.int32, (c, c), 1)
    incl = (ii <= jj) if rev else (ii >= jj)
    strict = (ii < jj) if rev else (ii > jj)
    dec_incl = jnp.exp(jnp.where(incl, gc_col - gc_row, NEG_INF))
    dec_strict = jnp.where(strict, dec_incl, 0.0)
    kb = k * beta_col
    kf = k.astype(BF16)
    a_kk = _dot_nt(kb.astype(BF16), kf) * dec_strict
    a_qk = _dot_nt(q.astype(BF16), kf) * dec_incl
    xp = -a_kk
    inv = jnp.where(ii == jj, 1.0, 0.0) + xp
    n_sq = int(math.log2(c)) - 1
    for _ in range(n_sq):
        xb = xp.astype(BF16)
        xp = _dot(xb, xb)
        inv = inv + _dot(inv.astype(BF16), xp.astype(BF16))
    eg = jnp.exp(gc_col)
    rhs = jnp.concatenate([v * beta_col, kb * eg], axis=-1).astype(BF16)
    sol = _dot(inv.astype(BF16), rhs)
    dv = v.shape[-1]
    u, w = sol[:, :dv], sol[:, dv:]
    sb = state.astype(BF16)
    v_new = u - _dot(w.astype(BF16), sb)
    vb = v_new.astype(BF16)
    o = _dot((q * eg).astype(BF16), sb) + _dot(a_qk.astype(BF16), vb)
    g_last = gc_col[0:1, :] if rev else gc_col[c - 1:c, :]
    kdec = k * jnp.exp(g_last - gc_col)
    new_state = jnp.exp(g_last) * state + _dot(kdec.T.astype(BF16), vb)
    return o, new_state


def _gdn_scan_kernel(qf_ref, kf_ref, vf_ref, abf_ref, qb_ref, kb_ref, vb_ref, abb_ref, alog_ref, dt_ref,
                     of_ref, ob_ref, st_ref, *, hb, n_heads):
    hg = pl.program_id(1)

    @pl.when(pl.program_id(2) == 0)
    def _():
        st_ref[...] = jnp.zeros_like(st_ref)

    neg_a = -jnp.exp(alog_ref[...])
    dt = dt_ref[...]
    for d_i, (q_ref, k_ref, v_ref, ab_ref, o_ref) in enumerate(
            ((qf_ref, kf_ref, vf_ref, abf_ref, of_ref), (qb_ref, kb_ref, vb_ref, abb_ref, ob_ref))):
        rev = d_i == 1
        ab = ab_ref[...]
        xa = ab + dt
        softplus = jnp.maximum(xa, 0.0) + jnp.log1p(jnp.exp(-jnp.abs(xa)))
        gc = _cumsum_rows(neg_a * softplus, rev)
        gc_t = gc.T
        beta = _sigmoid(ab)
        for h in range(hb):
            sl = slice(h * HEAD_DIM, (h + 1) * HEAD_DIM)
            lane = lax.broadcasted_iota(jnp.int32, (1, LANES), 1)
            head = hg * hb + h
            g_lane = d_i * n_heads + head
            b_lane = (2 + d_i) * n_heads + head
            gc_col = jnp.sum(jnp.where(lane == g_lane, gc, 0.0), axis=-1, keepdims=True)
            beta_col = jnp.sum(jnp.where(lane == b_lane, beta, 0.0), axis=-1, keepdims=True)
            sub = lax.broadcasted_iota(jnp.int32, (LANES, 1), 0)
            gc_row = jnp.sum(jnp.where(sub == g_lane, gc_t, 0.0), axis=0, keepdims=True)
            o, new_state = _gdn_chunk(q_ref[:, sl], k_ref[:, sl], v_ref[:, sl], gc_col, gc_row, beta_col,
                                      st_ref[d_i, h], rev)
            o_ref[:, sl] = o
            st_ref[d_i, h] = new_state


def _chunk_maps(nc, n):
    def fwd(s):
        return s

    def bwd(s):
        return jnp.where(s < nc, nc - 1 - s, n - 1 - s + nc)

    return fwd, bwd


def _gdn_scan(qkv, proj, a_log, dt_bias, lay, n_ctx, hb):
    bsz, s, _ = proj.shape
    gdn_w = lay["gdn_w"]
    n_heads = gdn_w // HEAD_DIM
    nhg = n_heads // hb
    tc = hb * HEAD_DIM
    n = s // CHUNK
    cf, cb = _chunk_maps(n_ctx // CHUNK, n)
    ab_blk = lay["ab"] // LANES
    pad = LANES - 2 * n_heads
    alog = jnp.concatenate([a_log.reshape(-1), jnp.zeros((pad,), F32)]).reshape(1, LANES)
    dtb = jnp.concatenate([dt_bias.reshape(-1), jnp.zeros((pad,), F32)]).reshape(1, LANES)

    def qkv_spec(kind, cm):
        return pl.BlockSpec((None, CHUNK, tc), lambda b, g, t: (b, cm(t), kind * nhg + g))

    def ab_spec(cm):
        return pl.BlockSpec((None, CHUNK, LANES), lambda b, g, t: (b, cm(t), ab_blk))

    par = pl.BlockSpec((1, LANES), lambda b, g, t: (0, 0))
    out_shape = jax.ShapeDtypeStruct((bsz, s, gdn_w), F32)
    return pl.pallas_call(
        functools.partial(_gdn_scan_kernel, hb=hb, n_heads=n_heads),
        out_shape=(out_shape, out_shape),
        grid=(bsz, nhg, n),
        in_specs=[qkv_spec(0, cf), qkv_spec(1, cf), qkv_spec(2, cf), ab_spec(cf),
                  qkv_spec(0, cb), qkv_spec(1, cb), qkv_spec(2, cb), ab_spec(cb), par, par],
        out_specs=(pl.BlockSpec((None, CHUNK, tc), lambda b, g, t: (b, cf(t), g)),
                   pl.BlockSpec((None, CHUNK, tc), lambda b, g, t: (b, cb(t), g))),
        scratch_shapes=[pltpu.VMEM((2, hb, HEAD_DIM, HEAD_DIM), F32)],
        compiler_params=_cp(("parallel", "parallel", "arbitrary")),
        name="gdn_scan",
    )(qkv, qkv, qkv, proj, qkv, qkv, qkv, proj, alog, dtb)


def _gla_intra(q, k, gc, rev):
    c = q.shape[0]
    nb = c // SUB
    jrow = lax.broadcasted_iota(jnp.int32, (c, 1), 0)
    col = lax.broadcasted_iota(jnp.int32, (1, c), 1)
    irow = lax.broadcasted_iota(jnp.int32, (SUB, 1), 0)
    blocks = []
    for r in range(nb):
        lo = r * SUB
        q_r, k_r, gc_r = q[lo:lo + SUB], k[lo:lo + SUB], gc[lo:lo + SUB]
        g_ref = gc[lo + SUB - 1:lo + SUB] if rev else gc[lo:lo + 1]
        qt = q_r * jnp.exp(gc_r - g_ref)
        earlier = (jrow >= lo + SUB) if rev else (jrow < lo)
        kt = jnp.where(earlier, k * jnp.exp(jnp.minimum(g_ref - gc, 0.0)), 0.0)
        a_r = _dot_nt(qt.astype(BF16), kt.astype(BF16))
        for t in range(SUB):
            valid = (irow <= t) if rev else (irow >= t)
            e = jnp.exp(jnp.where(valid, gc_r - gc_r[t:t + 1], 0.0))
            prod = jnp.where(valid, q_r * k_r[t:t + 1] * e, 0.0)
            colv = jnp.sum(prod, axis=-1, keepdims=True)
            a_r = a_r + jnp.where(col == lo + t, colv, 0.0)
        blocks.append(a_r)
    return jnp.concatenate(blocks, axis=0)


def _gla_chunk(qr, fr, iv, lb, log_lb, state_t, rev, zero_lb):
    c = qr.shape[0]
    q = _silu(qr) * HEAD_DIM ** -0.5
    ls = _log_sigmoid(fr)
    if zero_lb:
        log_f = ls
        k = _sigmoid(-fr)
    else:
        a, b = ls, log_lb + (ls - fr)
        log_f = jnp.maximum(a, b) + jnp.log1p(jnp.exp(-jnp.abs(a - b)))
        k = (1.0 - lb) * _sigmoid(-fr)
    gc = _cumsum_rows(log_f, rev)
    g_last = gc[0:1] if rev else gc[c - 1:c]
    sb = state_t.astype(BF16)
    a = _gla_intra(q, k, gc, rev)
    vb = iv.astype(BF16)
    o = _dot_nt((q * jnp.exp(gc)).astype(BF16), sb) + _dot(a.astype(BF16), vb)
    kdec = k * jnp.exp(g_last - gc)
    new_state = state_t * jnp.exp(g_last) + _dot(iv.T.astype(BF16), kdec.astype(BF16))
    return o, new_state


def _hgrn_scan_kernel(qf_ref, ff_ref, if_ref, qb_ref, fb_ref, ib_ref, lbl_ref, of_ref, ob_ref, st_ref,
                      *, hb, layer):
    @pl.when(pl.program_id(2) == 0)
    def _():
        st_ref[...] = jnp.zeros_like(st_ref)

    zero_lb = layer == 0
    if zero_lb:
        lb_all = log_lb_all = None
    else:
        logits = lbl_ref[...]
        p = jnp.exp(logits - jnp.max(logits, axis=0, keepdims=True))
        p = p / jnp.sum(p, axis=0, keepdims=True)
        lb_all = jnp.sum(p[1:layer + 1], axis=0, keepdims=True)
        log_lb_all = jnp.log(lb_all)
    for d_i, (q_ref, f_ref, i_ref, o_ref) in enumerate(
            ((qf_ref, ff_ref, if_ref, of_ref), (qb_ref, fb_ref, ib_ref, ob_ref))):
        for h in range(hb):
            sl = slice(h * HEAD_DIM, (h + 1) * HEAD_DIM)
            lb = None if zero_lb else lb_all[:, sl]
            llb = None if zero_lb else log_lb_all[:, sl]
            o, new_state = _gla_chunk(q_ref[:, sl], f_ref[:, sl], i_ref[:, sl], lb, llb, st_ref[d_i, h],
                                      d_i == 1, zero_lb)
            o_ref[:, sl] = o
            st_ref[d_i, h] = new_state


def _hgrn_scan(proj, lb_logits, lay, n_ctx, hb, layer):
    bsz, s, _ = proj.shape
    hg_w = lay["hg_w"]
    nhg = hg_w // HEAD_DIM // hb
    tc = hb * HEAD_DIM
    n = s // CHUNK
    cf, cb = _chunk_maps(n_ctx // CHUNK, n)
    depth = lb_logits.shape[0]

    def spec(off, cm):
        blk = off // tc
        return pl.BlockSpec((None, CHUNK, tc), lambda b, g, t: (b, cm(t), blk + g))

    out_shape = jax.ShapeDtypeStruct((bsz, s, hg_w), F32)
    return pl.pallas_call(
        functools.partial(_hgrn_scan_kernel, hb=hb, layer=layer),
        out_shape=(out_shape, out_shape),
        grid=(bsz, nhg, n),
        in_specs=[spec(lay["hq"], cf), spec(lay["hff"], cf), spec(lay["hi"], cf),
                  spec(lay["hq"], cb), spec(lay["hfb"], cb), spec(lay["hi"], cb),
                  pl.BlockSpec((depth, tc), lambda b, g, t: (0, g))],
        out_specs=(pl.BlockSpec((None, CHUNK, tc), lambda b, g, t: (b, cf(t), g)),
                   pl.BlockSpec((None, CHUNK, tc), lambda b, g, t: (b, cb(t), g))),
        scratch_shapes=[pltpu.VMEM((2, hb, HEAD_DIM, HEAD_DIM), F32)],
        compiler_params=_cp(("parallel", "parallel", "arbitrary")),
        name="hgrn_scan",
    )(proj, proj, proj, proj, proj, proj, lb_logits)


def _att_prep_kernel(q_ref, k_ref, v_ref, cos_ref, sa_ref, sb_ref, gq_ref, gk_ref, qo_ref, ko_ref, vo_ref):
    cos, sa, sb = cos_ref[...], sa_ref[...], sb_ref[...]

    def norm_rope(x, g):
        y = x * lax.rsqrt(jnp.mean(x * x, axis=-1, keepdims=True) + EPS) * g
        return y * cos + pltpu.roll(y, HEAD_DIM - HEAD_DIM // 4, 1) * sa + pltpu.roll(y, HEAD_DIM // 4, 1) * sb

    for h in range(q_ref.shape[-1] // HEAD_DIM):
        sl = slice(h * HEAD_DIM, (h + 1) * HEAD_DIM)
        qo_ref[:, sl] = norm_rope(q_ref[:, sl], gq_ref[...]).astype(qo_ref.dtype)
    for h in range(k_ref.shape[-1] // HEAD_DIM):
        sl = slice(h * HEAD_DIM, (h + 1) * HEAD_DIM)
        ko_ref[:, sl] = norm_rope(k_ref[:, sl], gk_ref[...]).astype(ko_ref.dtype)
    vo_ref[...] = v_ref[...].astype(vo_ref.dtype)


def _att_prep(proj, tables, gq, gk, lay, ts):
    bsz, s, _ = proj.shape
    att_w, kv_w = lay["att_w"], lay["kv_w"]
    row = lambda w, off: pl.BlockSpec((None, ts, w), lambda b, j: (b, j, off // w))
    tab = pl.BlockSpec((ts, HEAD_DIM), lambda b, j: (j, 0))
    gain = pl.BlockSpec((1, HEAD_DIM), lambda b, j: (0, 0))
    out = lambda w: pl.BlockSpec((None, ts, w), lambda b, j: (b, j, 0))
    return pl.pallas_call(
        _att_prep_kernel,
        out_shape=(jax.ShapeDtypeStruct((bsz, s, att_w), BF16), jax.ShapeDtypeStruct((bsz, s, kv_w), BF16),
                   jax.ShapeDtypeStruct((bsz, s, kv_w), BF16)),
        grid=(bsz, s // ts),
        in_specs=[row(att_w, lay["aq"]), row(kv_w, lay["ak"]), row(kv_w, lay["av"]), tab, tab, tab, gain, gain],
        out_specs=(out(att_w), out(kv_w), out(kv_w)),
        compiler_params=_cp(("parallel", "parallel")),
        name="att_prep",
    )(proj, proj, proj, *tables, gq.reshape(1, HEAD_DIM), gk.reshape(1, HEAD_DIM))


def _att_kernel(q_ref, k_ref, v_ref, o_ref):
    k, v = k_ref[...], v_ref[...]
    for h in range(q_ref.shape[-1] // HEAD_DIM):
        sl = slice(h * HEAD_DIM, (h + 1) * HEAD_DIM)
        s = _dot_nt(q_ref[:, sl], k) * HEAD_DIM ** -0.5
        p = jnp.exp(s - jnp.max(s, axis=-1, keepdims=True))
        denom = jnp.sum(p, axis=-1, keepdims=True)
        o_ref[:, sl] = _dot(p.astype(BF16), v) / denom


def _attention(qn, kn, vn, out_rows, tq, q_tile0, n_q_tiles, kv_len, group):
    bsz, _, att_w = qn.shape
    kv_heads = kn.shape[-1] // HEAD_DIM
    gw = group * HEAD_DIM
    return pl.pallas_call(
        _att_kernel,
        out_shape=jax.ShapeDtypeStruct((bsz, out_rows, att_w), F32),
        grid=(bsz, kv_heads, n_q_tiles),
        in_specs=[
            pl.BlockSpec((None, tq, gw), lambda b, g, i: (b, i + q_tile0, g)),
            pl.BlockSpec((None, kv_len, HEAD_DIM), lambda b, g, i: (b, 0, g)),
            pl.BlockSpec((None, kv_len, HEAD_DIM), lambda b, g, i: (b, 0, g)),
        ],
        out_specs=pl.BlockSpec((None, tq, gw), lambda b, g, i: (b, i, g)),
        compiler_params=_cp(("parallel", "parallel", "parallel")),
        name="attention",
    )(qn, kn, vn)


def _finalize_kernel(gof_ref, gob_ref, gz_ref, ao_ref, hof_ref, hob_ref, hz_ref, gg_ref, hgn_ref, y_ref,
                     *, gdn_w, att_w, hg_w):
    def gated(o, z, g):
        return o * lax.rsqrt(jnp.mean(o * o, axis=-1, keepdims=True) + EPS) * g * _silu(z)

    for h in range(gdn_w // HEAD_DIM):
        sl = slice(h * HEAD_DIM, (h + 1) * HEAD_DIM)
        y_ref[:, sl] = gated(gof_ref[:, sl] + gob_ref[:, sl], gz_ref[:, sl], gg_ref[...]).astype(y_ref.dtype)
    y_ref[:, gdn_w:gdn_w + att_w] = ao_ref[...].astype(y_ref.dtype)
    base = gdn_w + att_w
    for h in range(hg_w // HEAD_DIM):
        sl = slice(h * HEAD_DIM, (h + 1) * HEAD_DIM)
        osl = slice(base + h * HEAD_DIM, base + (h + 1) * HEAD_DIM)
        y_ref[:, osl] = gated(hof_ref[:, sl] + hob_ref[:, sl], hz_ref[:, sl], hgn_ref[...]).astype(y_ref.dtype)


def _finalize(gof, gob, att_lat, att_ctx, hof, hob, proj, gdn_g, hg_g, lay, ts, ncb, row_tile0):
    bsz, s, _ = proj.shape
    gdn_w, att_w, hg_w = lay["gdn_w"], lay["att_w"], lay["hg_w"]
    d = gdn_w + att_w + hg_w
    nt = s // ts - row_tile0
    rows = lambda w, off: pl.BlockSpec((None, ts, w), lambda b, j: (b, j + row_tile0, off // w))
    gain = pl.BlockSpec((1, HEAD_DIM), lambda b, j: (0, 0))
    if att_ctx is None:
        att = att_lat
        att_spec = pl.BlockSpec((None, ts, att_w), lambda b, j: (b, j + row_tile0 - ncb, 0))
    else:
        att = jnp.concatenate([att_ctx, att_lat], axis=1)
        att_spec = pl.BlockSpec((None, ts, att_w), lambda b, j: (b, j + row_tile0, 0))
    return pl.pallas_call(
        functools.partial(_finalize_kernel, gdn_w=gdn_w, att_w=att_w, hg_w=hg_w),
        out_shape=jax.ShapeDtypeStruct((bsz, s, d), BF16),
        grid=(bsz, nt),
        in_specs=[rows(gdn_w, 0), rows(gdn_w, 0), rows(gdn_w, lay["gz"]), att_spec,
                  rows(hg_w, 0), rows(hg_w, 0), rows(hg_w, lay["hg"]), gain, gain],
        out_specs=pl.BlockSpec((None, ts, d), lambda b, j: (b, j + row_tile0, 0)),
        compiler_params=_cp(("parallel", "parallel")),
        name="mixer_finalize",
    )(gof, gob, proj, att, hof, hob, proj, gdn_g.reshape(1, HEAD_DIM), hg_g.reshape(1, HEAD_DIM))


def _router_kernel(x_ref, g_ref, sh_ref, sc_ref, whi_ref, wlo_ref, rb_ref, h_ref, ri_ref, rw_ref, cnt_ref,
                   run_ref, *, n_groups, epg):
    @pl.when((pl.program_id(0) == 0) & (pl.program_id(1) == 0))
    def _():
        run_ref[...] = jnp.zeros_like(run_ref)

    x = x_ref[...]
    y = x * lax.rsqrt(jnp.mean(x * x, axis=-1, keepdims=True) + EPS) * g_ref[...]
    h = y * (1.0 + sc_ref[...]) + sh_ref[...]
    h_ref[...] = h
    h_hi = h.astype(BF16)
    h_lo = (h - h_hi.astype(F32)).astype(BF16)
    whi = whi_ref[...]
    logits = _dot(h_hi, whi) + _dot(h_hi, wlo_ref[...]) + _dot(h_lo, whi) + rb_ref[...]
    ts = logits.shape[0]
    lane = lax.broadcasted_iota(jnp.int32, (ts, LANES), 1)
    lane_f = lane.astype(F32)
    gmask = lane < n_groups
    lg = jnp.where(gmask, logits, NEG_INF)
    gmax = jnp.max(lg, axis=-1, keepdims=True)
    grp = jnp.min(jnp.where(lg == gmax, lane_f, float(LANES)), axis=-1, keepdims=True)
    zg = jnp.sum(jnp.where(gmask, jnp.exp(jnp.where(gmask, logits - gmax, 0.0)), 0.0), axis=-1, keepdims=True)
    p_grp = 1.0 / zg
    lo = float(n_groups) + grp * float(epg)
    emask = (lane_f >= lo) & (lane_f < lo + float(epg))
    le = jnp.where(emask, logits, NEG_INF)
    m1 = jnp.max(le, axis=-1, keepdims=True)
    i1 = jnp.min(jnp.where(le == m1, lane_f, float(LANES)), axis=-1, keepdims=True)
    le2 = jnp.where(lane_f == i1, NEG_INF, le)
    m2 = jnp.max(le2, axis=-1, keepdims=True)
    i2 = jnp.min(jnp.where(le2 == m2, lane_f, float(LANES)), axis=-1, keepdims=True)
    e2 = jnp.exp(m2 - m1)
    w1 = p_grp / (1.0 + e2)
    w2 = p_grp * e2 / (1.0 + e2)
    hot1 = lane_f == i1
    hot2 = lane_f == i2
    onehot = jnp.where(hot1 | hot2, 1.0, 0.0)
    ti = lax.broadcasted_iota(jnp.int32, (ts, ts), 0)
    tj = lax.broadcasted_iota(jnp.int32, (ts, ts), 1)
    tri = jnp.where(ti > tj, 1.0, 0.0).astype(BF16)
    before = _dot(tri, onehot.astype(BF16)) + run_ref[0:1, :]
    r1 = jnp.sum(jnp.where(hot1, before, 0.0), axis=-1, keepdims=True)
    r2 = jnp.sum(jnp.where(hot2, before, 0.0), axis=-1, keepdims=True)
    run_ref[...] = run_ref[...] + jnp.sum(onehot, axis=0, keepdims=True)
    cnt_ref[...] = run_ref[...]
    ng = float(n_groups)
    ri = jnp.where(lane == 0, i1 - ng, jnp.where(lane == 1, i2 - ng, jnp.where(lane == 2, r1, jnp.where(
        lane == 3, r2, 0.0))))
    ri_ref[...] = ri.astype(jnp.int32)
    rw_ref[...] = jnp.where(lane == 0, w1, jnp.where(lane == 1, w2, 0.0))


def _router(x, g, mod4, shift_chunk, scale_chunk, w_hi, w_lo, rbias, ts, ncb, row_tile0):
    bsz, s, d = x.shape
    ctx_row = bsz
    nt = s // ts - row_tile0
    t_moe = bsz * nt * ts
    tok = lambda b, j: (b * nt + j, 0)
    return pl.pallas_call(
        functools.partial(_router_kernel, n_groups=N_GROUPS, epg=EXPERTS_PER_GROUP),
        out_shape=(jax.ShapeDtypeStruct((t_moe, d), F32), jax.ShapeDtypeStruct((t_moe, LANES), jnp.int32),
                   jax.ShapeDtypeStruct((t_moe, LANES), F32), jax.ShapeDtypeStruct((8, LANES), F32)),
        grid=(bsz, nt),
        in_specs=[
            pl.BlockSpec((None, ts, d), lambda b, j: (b, j + row_tile0, 0)),
            pl.BlockSpec((1, d), lambda b, j: (0, 0)),
            pl.BlockSpec((None, 1, d), lambda b, j: (_mod_row(b, j + row_tile0, ncb, ctx_row), 0, shift_chunk)),
            pl.BlockSpec((None, 1, d), lambda b, j: (_mod_row(b, j + row_tile0, ncb, ctx_row), 0, scale_chunk)),
            pl.BlockSpec((d, LANES), lambda b, j: (0, 0)),
            pl.BlockSpec((d, LANES), lambda b, j: (0, 0)),
            pl.BlockSpec((1, LANES), lambda b, j: (0, 0)),
        ],
        out_specs=(pl.BlockSpec((ts, d), tok), pl.BlockSpec((ts, LANES), tok), pl.BlockSpec((ts, LANES), tok),
                   pl.BlockSpec((8, LANES), lambda b, j: (0, 0))),
        scratch_shapes=[pltpu.VMEM((8, LANES), F32)],
        compiler_params=_cp(("arbitrary", "arbitrary")),
        name="moe_router",
    )(x, g.reshape(1, d), mod4, mod4, w_hi, w_lo, rbias)


def _slotmap_kernel(pos_ref, src_ref, *, n_assign, n_slots):
    def init(i, c):
        src_ref[i] = 0
        return c

    lax.fori_loop(0, n_slots, init, 0)

    def body(a, c):
        src_ref[pos_ref[a]] = lax.shift_right_logical(a, 1)
        return c

    lax.fori_loop(0, n_assign, body, 0)


def _slotmap(pos_flat, n_slots):
    n_assign = pos_flat.shape[0]
    return pl.pallas_call(
        functools.partial(_slotmap_kernel, n_assign=n_assign, n_slots=n_slots),
        out_shape=jax.ShapeDtypeStruct((n_slots,), jnp.int32),
        in_specs=[pl.BlockSpec(memory_space=pltpu.SMEM)],
        out_specs=pl.BlockSpec(memory_space=pltpu.SMEM),
        name="moe_slotmap",
    )(pos_flat)


def _ffn_kernel(te_ref, nused_ref, src_ref, h_hbm, wg_ref, wu_ref, wd_ref, o_ref, xbuf, sem, *, tm):
    i = pl.program_id(0)

    @pl.when(i < nused_ref[0])
    def _():
        base = i * tm

        def row_copy(r, tok):
            return pltpu.make_async_copy(h_hbm.at[pl.ds(tok, 1)], xbuf.at[pl.ds(r, 1)], sem)

        def issue(r, c):
            row_copy(r, src_ref[base + r]).start()
            return c

        lax.fori_loop(0, tm, issue, 0)

        def wait(r, c):
            row_copy(r, 0).wait()
            return c

        lax.fori_loop(0, tm, wait, 0)
        x = xbuf[...].astype(BF16)
        hg = _dot(x, wg_ref[...])
        hu = _dot(x, wu_ref[...])
        o_ref[...] = _dot((_silu(hg) * hu).astype(BF16), wd_ref[...])


def _expert_ffn(h2, w_gate, w_up, w_down, tile_expert, n_used, src, n_tiles):
    t_moe, d = h2.shape
    n_exp, _, ff = w_gate.shape
    tm = MOE_TILE
    grid_spec = pltpu.PrefetchScalarGridSpec(
        num_scalar_prefetch=3,
        grid=(n_tiles,),
        in_specs=[
            pl.BlockSpec(memory_space=pl.ANY),
            pl.BlockSpec((None, d, ff), lambda i, te, nu, sr: (te[i], 0, 0)),
            pl.BlockSpec((None, d, ff), lambda i, te, nu, sr: (te[i], 0, 0)),
            pl.BlockSpec((None, ff, d), lambda i, te, nu, sr: (te[i], 0, 0)),
        ],
        out_specs=pl.BlockSpec((tm, d), lambda i, te, nu, sr: (i, 0)),
        scratch_shapes=[pltpu.VMEM((tm, d), F32), pltpu.SemaphoreType.DMA],
    )
    return pl.pallas_call(
        functools.partial(_ffn_kernel, tm=tm),
        out_shape=jax.ShapeDtypeStruct((n_tiles * tm, d), F32),
        grid_spec=grid_spec,
        compiler_params=_cp(("arbitrary",)),
        name="moe_ffn",
    )(tile_expert, n_used, src, h2, w_gate, w_up, w_down)


def _combine_kernel(pos_ref, x_ref, gate_ref, rw_ref, fg_ref, ys_hbm, o_ref, buf, sem, *, ts, nt, final):
    base = (pl.program_id(0) * nt + pl.program_id(1)) * ts

    def row_copy(r, kk, slot):
        return pltpu.make_async_copy(ys_hbm.at[pl.ds(slot, 1)], buf.at[kk, pl.ds(r, 1)], sem)

    def issue(r, c):
        for kk in range(2):
            row_copy(r, kk, pos_ref[(base + r) * 2 + kk]).start()
        return c

    lax.fori_loop(0, ts, issue, 0)

    def wait(r, c):
        for kk in range(2):
            row_copy(r, kk, 0).wait()
        return c

    lax.fori_loop(0, ts, wait, 0)
    rw = rw_ref[...]
    y = rw[:, 0:1] * buf[0] + rw[:, 1:2] * buf[1]
    xn = x_ref[...] + gate_ref[...] * y
    if final:
        xn = xn * lax.rsqrt(jnp.mean(xn * xn, axis=-1, keepdims=True) + EPS) * fg_ref[...]
    o_ref[...] = xn


def _combine(x, mod4, gate_chunk, rw, ys, pos_flat, final_g, ts, ncb, row_tile0, final):
    bsz, s, d = x.shape
    ctx_row = bsz
    nt = s // ts - row_tile0
    out_rows = nt * ts if final else s
    out_tile0 = 0 if final else row_tile0
    grid_spec = pltpu.PrefetchScalarGridSpec(
        num_scalar_prefetch=1,
        grid=(bsz, nt),
        in_specs=[
            pl.BlockSpec((None, ts, d), lambda b, j, p: (b, j + row_tile0, 0)),
            pl.BlockSpec((None, 1, d), lambda b, j, p: (_mod_row(b, j + row_tile0, ncb, ctx_row), 0, gate_chunk)),
            pl.BlockSpec((ts, LANES), lambda b, j, p: (b * nt + j, 0)),
            pl.BlockSpec((1, d), lambda b, j, p: (0, 0)),
            pl.BlockSpec(memory_space=pl.ANY),
        ],
        out_specs=pl.BlockSpec((None, ts, d), lambda b, j, p: (b, j + out_tile0, 0)),
        scratch_shapes=[pltpu.VMEM((2, ts, d), F32), pltpu.SemaphoreType.DMA],
    )
    return pl.pallas_call(
        functools.partial(_combine_kernel, ts=ts, nt=nt, final=final),
        out_shape=jax.ShapeDtypeStruct((bsz, out_rows, d), F32),
        grid_spec=grid_spec,
        compiler_params=_cp(("arbitrary", "arbitrary")),
        name="moe_combine",
    )(pos_flat, x, mod4, rw, final_g.reshape(1, d), ys)


def _in_layout(gdn_w, att_w, kv_w, hg_w, tn):
    lay = {"gdn_w": gdn_w, "att_w": att_w, "kv_w": kv_w, "hg_w": hg_w}
    cur = 0
    for name, w in (("gq", gdn_w), ("gk", gdn_w), ("gv", gdn_w), ("gz", gdn_w), ("aq", att_w), ("ak", kv_w),
                    ("av", kv_w), ("ab", LANES), ("hq", hg_w), ("hff", hg_w), ("hfb", hg_w), ("hi", hg_w),
                    ("hg", hg_w)):
        cur = _round_up(cur, w)
        lay[name] = cur
        cur += w
    lay["n"] = _round_up(cur, tn)
    return lay


def _prep_w_in(w_in_l, lay, n_gdn_heads):
    d = w_in_l.shape[0]
    gdn_w, att_w, kv_w, hg_w = lay["gdn_w"], lay["att_w"], lay["kv_w"], lay["hg_w"]
    sizes = (gdn_w, gdn_w, gdn_w, gdn_w, 4 * n_gdn_heads, att_w, kv_w, kv_w, hg_w, hg_w, hg_w, hg_w, hg_w)
    names = ("gq", "gk", "gv", "gz", "ab", "aq", "ak", "av", "hq", "hff", "hfb", "hi", "hg")
    offs = [0]
    for sz in sizes:
        offs.append(offs[-1] + sz)
    pieces = sorted((lay[nm], w_in_l[:, offs[i]:offs[i + 1]]) for i, nm in enumerate(names))
    out, cur = [], 0
    for start, piece in pieces:
        if start > cur:
            out.append(jnp.zeros((d, start - cur), w_in_l.dtype))
        out.append(piece)
        cur = start + piece.shape[1]
    if lay["n"] > cur:
        out.append(jnp.zeros((d, lay["n"] - cur), w_in_l.dtype))
    return jnp.concatenate(out, axis=1).astype(BF16)


def _rope_tables(n_ctx, n_lat):
    rows = n_lat // GRID_W
    row = jnp.repeat(jnp.arange(rows, dtype=F32), GRID_W)
    col = jnp.tile(jnp.arange(GRID_W, dtype=F32), rows)
    half = HEAD_DIM // 2
    inv_freq = ROPE_THETA ** (-jnp.arange(0, half, 2, dtype=F32) / half)
    ang_r = row[:, None] * inv_freq[None, :]
    ang_c = col[:, None] * inv_freq[None, :]
    ang = jnp.concatenate([ang_r, ang_r, ang_c, ang_c], axis=-1)
    cos, sin = jnp.cos(ang), jnp.sin(ang)
    lane = jnp.arange(HEAD_DIM)
    low = (lane % half) < half // 2
    sa = jnp.where(low, -sin, 0.0)
    sb = jnp.where(low, 0.0, sin)
    pad1 = jnp.ones((n_ctx, HEAD_DIM), F32)
    pad0 = jnp.zeros((n_ctx, HEAD_DIM), F32)
    return (jnp.concatenate([pad1, cos]), jnp.concatenate([pad0, sa]), jnp.concatenate([pad0, sb]))


def _largest_divisor(n, cap):
    for c in range(min(n, cap), 0, -1):
        if n % c == 0:
            return c
    return 1


def kernel(x, c, ctx, c_ctx, ada_w, ada_b, norm1_g, norm2_g, w_in, gdn_conv_w, gdn_a_log, gdn_dt_bias,
           gdn_norm_g, attn_q_norm_g, attn_k_norm_g, hgrn_lb_logits, hgrn_norm_g, w_out,
           router_group_w, router_group_b, router_expert_w, router_expert_b,
           moe_w_gate, moe_w_up, moe_w_down, final_norm_g):
    bsz, n_lat, d = x.shape
    n_ctx = ctx.shape[1]
    s = n_ctx + n_lat
    depth = ada_w.shape[0]
    n_gdn = gdn_a_log.shape[-1]
    gdn_w = n_gdn * HEAD_DIM
    hg_w = hgrn_lb_logits.shape[-1]
    att_w = w_out.shape[1] - gdn_w - hg_w
    kv_w = (w_in.shape[-1] - 4 * gdn_w - 4 * n_gdn - att_w - 5 * hg_w) // 2
    group = att_w // kv_w
    n_exp = moe_w_gate.shape[1]
    assert n_exp == N_GROUPS * EXPERTS_PER_GROUP and N_GROUPS + n_exp <= LANES
    assert n_ctx % CHUNK == 0 and n_lat % CHUNK == 0 and n_lat % GRID_W == 0
    assert 4 * n_gdn <= LANES

    ts = math.gcd(math.gcd(n_ctx, n_lat), 256)
    ncb = n_ctx // ts
    hb_g = _largest_divisor(n_gdn, 4)
    hb_h = _largest_divisor(hg_w // HEAD_DIM, 4)
    tn_in = 2048
    lay = _in_layout(gdn_w, att_w, kv_w, hg_w, tn_in)
    tm_in = _largest_divisor(bsz * s // ts, 2) * ts

    rows = _round_up(bsz + 1, 8)
    cc = jnp.concatenate([c, c_ctx[None, :], jnp.zeros((rows - bsz - 1, d), F32)], axis=0)
    mod = _ada_modulation(cc, ada_w, ada_b)
    tables = _rope_tables(n_ctx, n_lat)

    xc = jnp.concatenate([ctx, x], axis=1)
    out = None
    for l in range(depth):
        last = l == depth - 1
        row_tile0 = ncb if last else 0
        mod4 = mod[l].reshape(rows, 1, ADA_CHUNKS * d)
        w_in_p = _prep_w_in(w_in[l], lay, n_gdn)

        h = _norm_mod(xc, norm1_g[l], mod4, 0, 1, ts, ncb)
        proj = _matmul(h.reshape(bsz * s, d), w_in_p, tm_in, tn_in).reshape(bsz, s, lay["n"])
        qkv = _gdn_prep(proj, gdn_conv_w[l], lay, n_ctx, hb_g)
        gof, gob = _gdn_scan(qkv, proj, gdn_a_log[l], gdn_dt_bias[l], lay, n_ctx, hb_g)
        hof, hob = _hgrn_scan(proj, hgrn_lb_logits, lay, n_ctx, hb_h, l)
        qn, kn, vn = _att_prep(proj, tables, attn_q_norm_g[l], attn_k_norm_g[l], lay, ts)
        att_lat = _attention(qn, kn, vn, n_lat, ts, ncb, n_lat // ts, s, group)
        att_ctx = None if last else _attention(qn, kn, vn, n_ctx, ts, 0, ncb, n_ctx, group)
        y = _finalize(gof, gob, att_lat, att_ctx, hof, hob, proj, gdn_norm_g[l], hgrn_norm_g[l], lay, ts, ncb,
                      row_tile0)
        xc = _out_proj(y, w_out[l].astype(BF16), xc, mod4, 2, ts, ncb, row_tile0)

        rw_cat = jnp.concatenate([router_group_w[l], router_expert_w[l],
                                  jnp.zeros((d, LANES - N_GROUPS - n_exp), F32)], axis=1)
        rw_hi = rw_cat.astype(BF16)
        rw_lo = (rw_cat - rw_hi.astype(F32)).astype(BF16)
        rbias = jnp.concatenate([router_group_b[l], router_expert_b[l],
                                 jnp.zeros((LANES - N_GROUPS - n_exp,), F32)]).reshape(1, LANES)
        h2, ri, rwts, cnt = _router(xc, norm2_g[l], mod4, 3, 4, rw_hi, rw_lo, rbias, ts, ncb, row_tile0)
        t_moe = h2.shape[0]
        counts = cnt[0, N_GROUPS:N_GROUPS + n_exp].astype(jnp.int32)
        padded = (counts + MOE_TILE - 1) // MOE_TILE * MOE_TILE
        ends = jnp.cumsum(padded)
        offs = ends - padded
        n_tiles = 2 * t_moe // MOE_TILE + n_exp
        tile_start = jnp.arange(n_tiles, dtype=jnp.int32) * MOE_TILE
        tile_expert = jnp.minimum(jnp.sum((tile_start[:, None] >= ends[None, :]).astype(jnp.int32), axis=1),
                                  n_exp - 1).astype(jnp.int32)
        n_used = (ends[-1:] // MOE_TILE).astype(jnp.int32)
        pos = (offs[ri[:, 0:2]] + ri[:, 2:4]).reshape(-1).astype(jnp.int32)
        src = _slotmap(pos, n_tiles * MOE_TILE)
        ys = _expert_ffn(h2, moe_w_gate[l].astype(BF16), moe_w_up[l].astype(BF16), moe_w_down[l].astype(BF16),
                         tile_expert, n_used, src, n_tiles)
        res = _combine(xc, mod4, 5, rwts, ys, pos, final_norm_g, ts, ncb, row_tile0, last)
        if last:
            out = res
        else:
            xc = res
    return out
```

```python
import functools
import math

import jax
import jax.numpy as jnp
from jax import lax
from jax.experimental import pallas as pl
from jax.experimental.pallas import tpu as pltpu

HEAD_DIM = 128
CHUNK = 64
SUB = 16
GRID_W = 64
ROPE_THETA = 10000.0
N_GROUPS = 4
EXPERTS_PER_GROUP = 8
ADA_CHUNKS = 6
EPS = 1e-6
LANES = 128
MOE_TILE = 256
VMEM_LIMIT = 56 * 1024 * 1024

F32 = jnp.float32
BF16 = jnp.bfloat16
NEG_INF = float("-inf")


def _cp(sem, vmem=VMEM_LIMIT):
    return pltpu.CompilerParams(dimension_semantics=sem, vmem_limit_bytes=vmem)


def _round_up(x, m):
    return (x + m - 1) // m * m


def _sigmoid(x):
    return 1.0 / (1.0 + jnp.exp(-x))


def _silu(x):
    return x * _sigmoid(x)


def _log_sigmoid(x):
    return jnp.minimum(x, 0.0) - jnp.log1p(jnp.exp(-jnp.abs(x)))


def _dot(a, b):
    return jnp.dot(a, b, preferred_element_type=F32)


def _dot_nt(a, b):
    return lax.dot_general(a, b, (((1,), (1,)), ((), ())), preferred_element_type=F32)


def _ada_kernel(c_ref, w_ref, b_ref, o_ref):
    c = c_ref[...]
    s = _silu(c).astype(BF16)
    o_ref[...] = _dot(s, w_ref[...].astype(BF16)) + b_ref[...]


def _ada_modulation(cc, ada_w, ada_b):
    depth, d, n = ada_w.shape
    rows = cc.shape[0]
    tn = 512 if n % 512 == 0 else LANES
    return pl.pallas_call(
        _ada_kernel,
        out_shape=jax.ShapeDtypeStruct((depth, rows, n), F32),
        grid=(depth, n // tn),
        in_specs=[
            pl.BlockSpec((rows, d), lambda l, j: (0, 0)),
            pl.BlockSpec((None, d, tn), lambda l, j: (l, 0, j)),
            pl.BlockSpec((None, 1, tn), lambda l, j: (l, 0, j)),
        ],
        out_specs=pl.BlockSpec((None, rows, tn), lambda l, j: (l, 0, j)),
        compiler_params=_cp(("parallel", "parallel")),
        name="ada_mod",
    )(cc, ada_w, ada_b.reshape(depth, 1, n))


def _norm_mod_kernel(x_ref, g_ref, sh_ref, sc_ref, o_ref):
    x = x_ref[...]
    y = x * lax.rsqrt(jnp.mean(x * x, axis=-1, keepdims=True) + EPS) * g_ref[...]
    o_ref[...] = (y * (1.0 + sc_ref[...]) + sh_ref[...]).astype(o_ref.dtype)


def _mod_row(b, j, ncb, ctx_row):
    return jnp.where(j < ncb, ctx_row, b)


def _norm_mod(x, g, mod4, shift_chunk, scale_chunk, ts, ncb):
    bsz, s, d = x.shape
    ctx_row = bsz
    return pl.pallas_call(
        _norm_mod_kernel,
        out_shape=jax.ShapeDtypeStruct((bsz, s, d), BF16),
        grid=(bsz, s // ts),
        in_specs=[
            pl.BlockSpec((None, ts, d), lambda b, j: (b, j, 0)),
            pl.BlockSpec((1, d), lambda b, j: (0, 0)),
            pl.BlockSpec((None, 1, d), lambda b, j: (_mod_row(b, j, ncb, ctx_row), 0, shift_chunk)),
            pl.BlockSpec((None, 1, d), lambda b, j: (_mod_row(b, j, ncb, ctx_row), 0, scale_chunk)),
        ],
        out_specs=pl.BlockSpec((None, ts, d), lambda b, j: (b, j, 0)),
        compiler_params=_cp(("parallel", "parallel")),
        name="norm_mod",
    )(x, g.reshape(1, d), mod4, mod4)


def _mm_kernel(a_ref, w_ref, o_ref):
    o_ref[...] = _dot(a_ref[...], w_ref[...]).astype(o_ref.dtype)


def _matmul(a, w, tm, tn):
    m, k = a.shape
    n = w.shape[1]
    return pl.pallas_call(
        _mm_kernel,
        out_shape=jax.ShapeDtypeStruct((m, n), F32),
        grid=(n // tn, m // tm),
        in_specs=[
            pl.BlockSpec((tm, k), lambda j, i: (i, 0)),
            pl.BlockSpec((k, tn), lambda j, i: (0, j)),
        ],
        out_specs=pl.BlockSpec((tm, tn), lambda j, i: (i, j)),
        compiler_params=_cp(("parallel", "parallel")),
        name="in_proj",
    )(a, w)


def _mm_res_kernel(a_ref, w_ref, x_ref, gate_ref, o_ref):
    o_ref[...] = x_ref[...] + gate_ref[...] * _dot(a_ref[...], w_ref[...])


def _out_proj(y, w, x, mod4, gate_chunk, ts, ncb, x_tile0):
    bsz, rows_out, k = y.shape
    d = x.shape[-1]
    tn = 2048 if d % 2048 == 0 else d
    ctx_row = bsz
    nt = rows_out // ts
    gpc = d // tn
    return pl.pallas_call(
        _mm_res_kernel,
        out_shape=jax.ShapeDtypeStruct((bsz, rows_out, d), F32),
        grid=(gpc, bsz, nt),
        in_specs=[
            pl.BlockSpec((None, ts, k), lambda n, b, i: (b, i, 0)),
            pl.BlockSpec((k, tn), lambda n, b, i: (0, n)),
            pl.BlockSpec((None, ts, tn), lambda n, b, i: (b, i + x_tile0, n)),
            pl.BlockSpec((None, 1, tn),
                         lambda n, b, i: (_mod_row(b, i + x_tile0, ncb, ctx_row), 0, gate_chunk * gpc + n)),
        ],
        out_specs=pl.BlockSpec((None, ts, tn), lambda n, b, i: (b, i, n)),
        compiler_params=_cp(("parallel", "parallel", "parallel")),
        name="out_proj",
    )(y, w, x, mod4)


def _gdn_prep_kernel(x_ref, w_ref, o_ref, *, n_ctx, gdn_w, conv_w):
    j = pl.program_id(1)
    x = x_ref[...]
    s, tc = x.shape
    pad = conv_w // 2
    row = lax.broadcasted_iota(jnp.int32, (s, 1), 0)
    is_lat = row >= n_ctx
    acc = jnp.zeros_like(x)
    for tap in range(conv_w):
        sft = pad - tap
        xs = x if sft == 0 else pltpu.roll(x, sft % s, 0)
        src = row - sft
        valid = (src >= 0) & (src < s) & ((src >= n_ctx) == is_lat)
        acc = acc + w_ref[tap:tap + 1, :] * jnp.where(valid, xs, 0.0)
    y = _silu(acc)
    kind = (j * tc) // gdn_w
    scale = jnp.where(kind == 0, HEAD_DIM ** -0.5, 1.0).astype(F32)
    normed = kind < 2
    for h in range(tc // HEAD_DIM):
        sl = slice(h * HEAD_DIM, (h + 1) * HEAD_DIM)
        yh = y[:, sl]
        nh = yh * lax.rsqrt(jnp.sum(yh * yh, axis=-1, keepdims=True) + EPS) * scale
        o_ref[:, sl] = jnp.where(normed, nh, yh)


def _gdn_prep(proj, conv_w_arr, lay, n_ctx, hb):
    bsz, s, _ = proj.shape
    gdn_w = lay["gdn_w"]
    tc = hb * HEAD_DIM
    kw = conv_w_arr.shape[0]
    return pl.pallas_call(
        functools.partial(_gdn_prep_kernel, n_ctx=n_ctx, gdn_w=gdn_w, conv_w=kw),
        out_shape=jax.ShapeDtypeStruct((bsz, s, 3 * gdn_w), F32),
        grid=(bsz, 3 * gdn_w // tc),
        in_specs=[
            pl.BlockSpec((None, s, tc), lambda b, j: (b, 0, j)),
            pl.BlockSpec((kw, tc), lambda b, j: (0, j)),
        ],
        out_specs=pl.BlockSpec((None, s, tc), lambda b, j: (b, 0, j)),
        compiler_params=_cp(("parallel", "parallel")),
        name="gdn_prep",
    )(proj, conv_w_arr)


def _cumsum_rows(x, rev):
    c = x.shape[0]
    row = lax.broadcasted_iota(jnp.int32, (c, 1), 0)
    step = 1
    while step < c:
        if rev:
            x = x + jnp.where(row < c - step, pltpu.roll(x, c - step, 0), 0.0)
        else:
            x = x + jnp.where(row >= step, pltpu.roll(x, step, 0), 0.0)
        step *= 2
    return x


def _gdn_chunks(chains):
    c = chains[0]["q"].shape[0]
    ii2 = lax.broadcasted_iota(jnp.int32, (c, 2 * c), 0)
    jj2 = lax.broadcasted_iota(jnp.int32, (c, 2 * c), 1)
    eye_hi = jnp.where(jj2 == ii2 + c, 1.0, 0.0)
    hi_half = jj2 >= c
    zeros_k = jnp.zeros((c, chains[0]["k"].shape[1]), BF16)
    st = []
    for ch in chains:
        rev, k, gc_col = ch["rev"], ch["k"], ch["gc_col"]
        incl2 = ((ii2 <= jj2) if rev else (ii2 >= jj2)) & (jj2 < c)
        dec_incl2 = jnp.exp(jnp.where(incl2, gc_col - ch["gc_row2"], NEG_INF))
        dec_strict2 = jnp.where(ii2 == jj2, 0.0, dec_incl2)
        kb = k * ch["beta_col"]
        kf2 = jnp.concatenate([k.astype(BF16), zeros_k], axis=0)
        st.append(dict(dec_incl2=dec_incl2, dec_strict2=dec_strict2, kb=kb, kf2=kf2, eg=jnp.exp(gc_col)))
    akk2 = [_dot_nt(s["kb"].astype(BF16), s["kf2"]) for s in st]
    aqk2 = [_dot_nt(ch["q"].astype(BF16), s["kf2"]) * s["dec_incl2"] for ch, s in zip(chains, st)]
    ws = [eye_hi - a * s["dec_strict2"] for a, s in zip(akk2, st)]
    for _ in range(int(math.log2(c))):
        ys = [_dot(w[:, :c].astype(BF16), w.astype(BF16)) for w in ws]
        ws = [y + jnp.where(hi_half, w, 0.0) for y, w in zip(ys, ws)]
    sols = []
    for ch, s, w in zip(chains, st, ws):
        rhs = jnp.concatenate([ch["v"] * ch["beta_col"], s["kb"] * s["eg"]], axis=-1).astype(BF16)
        rhs2 = jnp.concatenate([jnp.zeros_like(rhs), rhs], axis=0)
        sols.append((w.astype(BF16), rhs2))
    sols = [_dot(w, r) for w, r in sols]
    dv = chains[0]["v"].shape[-1]
    sbs = [ch["state"].astype(BF16) for ch in chains]
    ws_s = [_dot(sol[:, dv:].astype(BF16), sb) for sol, sb in zip(sols, sbs)]
    qs_s = [_dot((ch["q"] * s["eg"]).astype(BF16), sb) for ch, s, sb in zip(chains, st, sbs)]
    vbs = [(sol[:, :dv] - wss).astype(BF16) for sol, wss in zip(sols, ws_s)]
    outs = [qss + _dot(a[:, :c].astype(BF16), vb) for qss, a, vb in zip(qs_s, aqk2, vbs)]
    new_states = []
    for ch, vb in zip(chains, vbs):
        gc_col = ch["gc_col"]
        g_last = gc_col[0:1, :] if ch["rev"] else gc_col[c - 1:c, :]
        kdec = ch["k"] * jnp.exp(g_last - gc_col)
        new_states.append(jnp.exp(g_last) * ch["state"] + _dot(kdec.T.astype(BF16), vb))
    return outs, new_states


def _gdn_scan_kernel(qf_ref, kf_ref, vf_ref, abf_ref, qb_ref, kb_ref, vb_ref, abb_ref, alog_ref, dt_ref,
                     of_ref, ob_ref, st_ref, *, hb, n_heads):
    hg = pl.program_id(1)

    @pl.when(pl.program_id(2) == 0)
    def _():
        st_ref[...] = jnp.zeros_like(st_ref)

    neg_a = -jnp.exp(alog_ref[...])
    dt = dt_ref[...]
    lane = lax.broadcasted_iota(jnp.int32, (1, LANES), 1)
    sub = lax.broadcasted_iota(jnp.int32, (LANES, 1), 0)
    chains, dests = [], []
    for d_i, (q_ref, k_ref, v_ref, ab_ref, o_ref) in enumerate(
            ((qf_ref, kf_ref, vf_ref, abf_ref, of_ref), (qb_ref, kb_ref, vb_ref, abb_ref, ob_ref))):
        rev = d_i == 1
        ab = ab_ref[...]
        xa = ab + dt
        softplus = jnp.maximum(xa, 0.0) + jnp.log1p(jnp.exp(-jnp.abs(xa)))
        gc = _cumsum_rows(neg_a * softplus, rev)
        gc_t = jnp.concatenate([gc, jnp.zeros_like(gc)], axis=0).T
        beta = _sigmoid(ab)
        for h in range(hb):
            sl = slice(h * HEAD_DIM, (h + 1) * HEAD_DIM)
            head = hg * hb + h
            g_lane = d_i * n_heads + head
            b_lane = (2 + d_i) * n_heads + head
            chains.append(dict(
                q=q_ref[:, sl], k=k_ref[:, sl], v=v_ref[:, sl], rev=rev, state=st_ref[d_i, h],
                gc_col=jnp.sum(jnp.where(lane == g_lane, gc, 0.0), axis=-1, keepdims=True),
                beta_col=jnp.sum(jnp.where(lane == b_lane, beta, 0.0), axis=-1, keepdims=True),
                gc_row2=jnp.sum(jnp.where(sub == g_lane, gc_t, 0.0), axis=0, keepdims=True)))
            dests.append((o_ref, sl, d_i, h))
    outs, new_states = _gdn_chunks(chains)
    for (o_ref, sl, d_i, h), o, ns in zip(dests, outs, new_states):
        o_ref[:, sl] = o
        st_ref[d_i, h] = ns


def _chunk_maps(nc, n):
    def fwd(s):
        return s

    def bwd(s):
        return jnp.where(s < nc, nc - 1 - s, n - 1 - s + nc)

    return fwd, bwd


def _gdn_scan(qkv, proj, a_log, dt_bias, lay, n_ctx, hb):
    bsz, s, _ = proj.shape
    gdn_w = lay["gdn_w"]
    n_heads = gdn_w // HEAD_DIM
    nhg = n_heads // hb
    tc = hb * HEAD_DIM
    n = s // CHUNK
    cf, cb = _chunk_maps(n_ctx // CHUNK, n)
    ab_blk = lay["ab"] // LANES
    pad = LANES - 2 * n_heads
    alog = jnp.concatenate([a_log.reshape(-1), jnp.zeros((pad,), F32)]).reshape(1, LANES)
    dtb = jnp.concatenate([dt_bias.reshape(-1), jnp.zeros((pad,), F32)]).reshape(1, LANES)

    def qkv_spec(kind, cm):
        return pl.BlockSpec((None, CHUNK, tc), lambda b, g, t: (b, cm(t), kind * nhg + g))

    def ab_spec(cm):
        return pl.BlockSpec((None, CHUNK, LANES), lambda b, g, t: (b, cm(t), ab_blk))

    par = pl.BlockSpec((1, LANES), lambda b, g, t: (0, 0))
    out_shape = jax.ShapeDtypeStruct((bsz, s, gdn_w), F32)
    return pl.pallas_call(
        functools.partial(_gdn_scan_kernel, hb=hb, n_heads=n_heads),
        out_shape=(out_shape, out_shape),
        grid=(bsz, nhg, n),
        in_specs=[qkv_spec(0, cf), qkv_spec(1, cf), qkv_spec(2, cf), ab_spec(cf),
                  qkv_spec(0, cb), qkv_spec(1, cb), qkv_spec(2, cb), ab_spec(cb), par, par],
        out_specs=(pl.BlockSpec((None, CHUNK, tc), lambda b, g, t: (b, cf(t), g)),
                   pl.BlockSpec((None, CHUNK, tc), lambda b, g, t: (b, cb(t), g))),
        scratch_shapes=[pltpu.VMEM((2, hb, HEAD_DIM, HEAD_DIM), F32)],
        compiler_params=_cp(("parallel", "parallel", "arbitrary")),
        name="gdn_scan",
    )(qkv, qkv, qkv, proj, qkv, qkv, qkv, proj, alog, dtb)


def _gla_intra(q, k, gc, rev):
    c = q.shape[0]
    nb = c // SUB
    jrow = lax.broadcasted_iota(jnp.int32, (c, 1), 0)
    col = lax.broadcasted_iota(jnp.int32, (1, c), 1)
    irow = lax.broadcasted_iota(jnp.int32, (SUB, 1), 0)
    blocks = []
    for r in range(nb):
        lo = r * SUB
        q_r, k_r, gc_r = q[lo:lo + SUB], k[lo:lo + SUB], gc[lo:lo + SUB]
        g_ref = gc[lo + SUB - 1:lo + SUB] if rev else gc[lo:lo + 1]
        qt = q_r * jnp.exp(gc_r - g_ref)
        earlier = (jrow >= lo + SUB) if rev else (jrow < lo)
        kt = jnp.where(earlier, k * jnp.exp(jnp.minimum(g_ref - gc, 0.0)), 0.0)
        a_r = _dot_nt(qt.astype(BF16), kt.astype(BF16))
        for t in range(SUB):
            valid = (irow <= t) if rev else (irow >= t)
            e = jnp.exp(jnp.where(valid, gc_r - gc_r[t:t + 1], 0.0))
            prod = jnp.where(valid, q_r * k_r[t:t + 1] * e, 0.0)
            colv = jnp.sum(prod, axis=-1, keepdims=True)
            a_r = a_r + jnp.where(col == lo + t, colv, 0.0)
        blocks.append(a_r)
    return jnp.concatenate(blocks, axis=0)


def _gla_chunk(qr, fr, iv, lb, log_lb, state_t, rev, zero_lb):
    c = qr.shape[0]
    q = _silu(qr) * HEAD_DIM ** -0.5
    ls = _log_sigmoid(fr)
    if zero_lb:
        log_f = ls
        k = _sigmoid(-fr)
    else:
        a, b = ls, log_lb + (ls - fr)
        log_f = jnp.maximum(a, b) + jnp.log1p(jnp.exp(-jnp.abs(a - b)))
        k = (1.0 - lb) * _sigmoid(-fr)
    gc = _cumsum_rows(log_f, rev)
    g_last = gc[0:1] if rev else gc[c - 1:c]
    sb = state_t.astype(BF16)
    a = _gla_intra(q, k, gc, rev)
    vb = iv.astype(BF16)
    o = _dot_nt((q * jnp.exp(gc)).astype(BF16), sb) + _dot(a.astype(BF16), vb)
    kdec = k * jnp.exp(g_last - gc)
    new_state = state_t * jnp.exp(g_last) + _dot(iv.T.astype(BF16), kdec.astype(BF16))
    return o, new_state


def _hgrn_scan_kernel(qf_ref, ff_ref, if_ref, qb_ref, fb_ref, ib_ref, lbl_ref, of_ref, ob_ref, st_ref,
                      *, hb, layer):
    @pl.when(pl.program_id(2) == 0)
    def _():
        st_ref[...] = jnp.zeros_like(st_ref)

    zero_lb = layer == 0
    if zero_lb:
        lb_all = log_lb_all = None
    else:
        logits = lbl_ref[...]
        p = jnp.exp(logits - jnp.max(logits, axis=0, keepdims=True))
        p = p / jnp.sum(p, axis=0, keepdims=True)
        lb_all = jnp.sum(p[1:layer + 1], axis=0, keepdims=True)
        log_lb_all = jnp.log(lb_all)
    for d_i, (q_ref, f_ref, i_ref, o_ref) in enumerate(
            ((qf_ref, ff_ref, if_ref, of_ref), (qb_ref, fb_ref, ib_ref, ob_ref))):
        for h in range(hb):
            sl = slice(h * HEAD_DIM, (h + 1) * HEAD_DIM)
            lb = None if zero_lb else lb_all[:, sl]
            llb = None if zero_lb else log_lb_all[:, sl]
            o, new_state = _gla_chunk(q_ref[:, sl], f_ref[:, sl], i_ref[:, sl], lb, llb, st_ref[d_i, h],
                                      d_i == 1, zero_lb)
            o_ref[:, sl] = o
            st_ref[d_i, h] = new_state


def _hgrn_scan(proj, lb_logits, lay, n_ctx, hb, layer):
    bsz, s, _ = proj.shape
    hg_w = lay["hg_w"]
    nhg = hg_w // HEAD_DIM // hb
    tc = hb * HEAD_DIM
    n = s // CHUNK
    cf, cb = _chunk_maps(n_ctx // CHUNK, n)
    depth = lb_logits.shape[0]

    def spec(off, cm):
        blk = off // tc
        return pl.BlockSpec((None, CHUNK, tc), lambda b, g, t: (b, cm(t), blk + g))

    out_shape = jax.ShapeDtypeStruct((bsz, s, hg_w), F32)
    return pl.pallas_call(
        functools.partial(_hgrn_scan_kernel, hb=hb, layer=layer),
        out_shape=(out_shape, out_shape),
        grid=(bsz, nhg, n),
        in_specs=[spec(lay["hq"], cf), spec(lay["hff"], cf), spec(lay["hi"], cf),
                  spec(lay["hq"], cb), spec(lay["hfb"], cb), spec(lay["hi"], cb),
                  pl.BlockSpec((depth, tc), lambda b, g, t: (0, g))],
        out_specs=(pl.BlockSpec((None, CHUNK, tc), lambda b, g, t: (b, cf(t), g)),
                   pl.BlockSpec((None, CHUNK, tc), lambda b, g, t: (b, cb(t), g))),
        scratch_shapes=[pltpu.VMEM((2, hb, HEAD_DIM, HEAD_DIM), F32)],
        compiler_params=_cp(("parallel", "parallel", "arbitrary")),
        name="hgrn_scan",
    )(proj, proj, proj, proj, proj, proj, lb_logits)


def _att_prep_kernel(q_ref, k_ref, v_ref, cos_ref, sa_ref, sb_ref, gq_ref, gk_ref, qo_ref, ko_ref, vo_ref):
    cos, sa, sb = cos_ref[...], sa_ref[...], sb_ref[...]

    def norm_rope(x, g):
        y = x * lax.rsqrt(jnp.mean(x * x, axis=-1, keepdims=True) + EPS) * g
        return y * cos + pltpu.roll(y, HEAD_DIM - HEAD_DIM // 4, 1) * sa + pltpu.roll(y, HEAD_DIM // 4, 1) * sb

    for h in range(q_ref.shape[-1] // HEAD_DIM):
        sl = slice(h * HEAD_DIM, (h + 1) * HEAD_DIM)
        qo_ref[:, sl] = norm_rope(q_ref[:, sl], gq_ref[...]).astype(qo_ref.dtype)
    for h in range(k_ref.shape[-1] // HEAD_DIM):
        sl = slice(h * HEAD_DIM, (h + 1) * HEAD_DIM)
        ko_ref[:, sl] = norm_rope(k_ref[:, sl], gk_ref[...]).astype(ko_ref.dtype)
    vo_ref[...] = v_ref[...].astype(vo_ref.dtype)


def _att_prep(proj, tables, gq, gk, lay, ts):
    bsz, s, _ = proj.shape
    att_w, kv_w = lay["att_w"], lay["kv_w"]
    row = lambda w, off: pl.BlockSpec((None, ts, w), lambda b, j: (b, j, off // w))
    tab = pl.BlockSpec((ts, HEAD_DIM), lambda b, j: (j, 0))
    gain = pl.BlockSpec((1, HEAD_DIM), lambda b, j: (0, 0))
    out = lambda w: pl.BlockSpec((None, ts, w), lambda b, j: (b, j, 0))
    return pl.pallas_call(
        _att_prep_kernel,
        out_shape=(jax.ShapeDtypeStruct((bsz, s, att_w), BF16), jax.ShapeDtypeStruct((bsz, s, kv_w), BF16),
                   jax.ShapeDtypeStruct((bsz, s, kv_w), BF16)),
        grid=(bsz, s // ts),
        in_specs=[row(att_w, lay["aq"]), row(kv_w, lay["ak"]), row(kv_w, lay["av"]), tab, tab, tab, gain, gain],
        out_specs=(out(att_w), out(kv_w), out(kv_w)),
        compiler_params=_cp(("parallel", "parallel")),
        name="att_prep",
    )(proj, proj, proj, *tables, gq.reshape(1, HEAD_DIM), gk.reshape(1, HEAD_DIM))


def _att_kernel(q_ref, k_ref, v_ref, o_ref, *, n_ctx, n_ctx_tiles):
    def attend(kv_len):
        k, v = k_ref[0:kv_len, :], v_ref[0:kv_len, :]
        for h in range(q_ref.shape[-1] // HEAD_DIM):
            sl = slice(h * HEAD_DIM, (h + 1) * HEAD_DIM)
            s = _dot_nt(q_ref[:, sl], k) * HEAD_DIM ** -0.5
            p = jnp.exp(s - jnp.max(s, axis=-1, keepdims=True))
            denom = jnp.sum(p, axis=-1, keepdims=True)
            o_ref[:, sl] = _dot(p.astype(BF16), v) / denom

    if n_ctx_tiles > 0:
        is_ctx = pl.program_id(2) < n_ctx_tiles

        @pl.when(is_ctx)
        def _():
            attend(n_ctx)

        @pl.when(jnp.logical_not(is_ctx))
        def _():
            attend(k_ref.shape[0])
    else:
        attend(k_ref.shape[0])


def _attention(qn, kn, vn, tq, q_tile0, n_ctx, group):
    bsz, s, att_w = qn.shape
    kv_heads = kn.shape[-1] // HEAD_DIM
    gw = group * HEAD_DIM
    n_q_tiles = s // tq - q_tile0
    n_ctx_tiles = max(n_ctx // tq - q_tile0, 0)
    return pl.pallas_call(
        functools.partial(_att_kernel, n_ctx=n_ctx, n_ctx_tiles=n_ctx_tiles),
        out_shape=jax.ShapeDtypeStruct((bsz, n_q_tiles * tq, att_w), F32),
        grid=(bsz, kv_heads, n_q_tiles),
        in_specs=[
            pl.BlockSpec((None, tq, gw), lambda b, g, i: (b, i + q_tile0, g)),
            pl.BlockSpec((None, s, HEAD_DIM), lambda b, g, i: (b, 0, g)),
            pl.BlockSpec((None, s, HEAD_DIM), lambda b, g, i: (b, 0, g)),
        ],
        out_specs=pl.BlockSpec((None, tq, gw), lambda b, g, i: (b, i, g)),
        compiler_params=_cp(("parallel", "parallel", "parallel")),
        name="attention",
    )(qn, kn, vn)


def _finalize_kernel(gof_ref, gob_ref, gz_ref, ao_ref, hof_ref, hob_ref, hz_ref, gg_ref, hgn_ref, y_ref,
                     *, gdn_w, att_w, hg_w):
    def gated(o, z, g):
        return o * lax.rsqrt(jnp.mean(o * o, axis=-1, keepdims=True) + EPS) * g * _silu(z)

    for h in range(gdn_w // HEAD_DIM):
        sl = slice(h * HEAD_DIM, (h + 1) * HEAD_DIM)
        y_ref[:, sl] = gated(gof_ref[:, sl] + gob_ref[:, sl], gz_ref[:, sl], gg_ref[...]).astype(y_ref.dtype)
    y_ref[:, gdn_w:gdn_w + att_w] = ao_ref[...].astype(y_ref.dtype)
    base = gdn_w + att_w
    for h in range(hg_w // HEAD_DIM):
        sl = slice(h * HEAD_DIM, (h + 1) * HEAD_DIM)
        osl = slice(base + h * HEAD_DIM, base + (h + 1) * HEAD_DIM)
        y_ref[:, osl] = gated(hof_ref[:, sl] + hob_ref[:, sl], hz_ref[:, sl], hgn_ref[...]).astype(y_ref.dtype)


def _finalize(gof, gob, att, hof, hob, proj, gdn_g, hg_g, lay, ts, in_tile0):
    bsz, s, _ = proj.shape
    gdn_w, att_w, hg_w = lay["gdn_w"], lay["att_w"], lay["hg_w"]
    d = gdn_w + att_w + hg_w
    nt = s // ts - in_tile0
    rows = lambda w, off: pl.BlockSpec((None, ts, w), lambda b, j: (b, j + in_tile0, off // w))
    gain = pl.BlockSpec((1, HEAD_DIM), lambda b, j: (0, 0))
    return pl.pallas_call(
        functools.partial(_finalize_kernel, gdn_w=gdn_w, att_w=att_w, hg_w=hg_w),
        out_shape=jax.ShapeDtypeStruct((bsz, nt * ts, d), BF16),
        grid=(bsz, nt),
        in_specs=[rows(gdn_w, 0), rows(gdn_w, 0), rows(gdn_w, lay["gz"]),
                  pl.BlockSpec((None, ts, att_w), lambda b, j: (b, j, 0)),
                  rows(hg_w, 0), rows(hg_w, 0), rows(hg_w, lay["hg"]), gain, gain],
        out_specs=pl.BlockSpec((None, ts, d), lambda b, j: (b, j, 0)),
        compiler_params=_cp(("parallel", "parallel")),
        name="mixer_finalize",
    )(gof, gob, proj, att, hof, hob, proj, gdn_g.reshape(1, HEAD_DIM), hg_g.reshape(1, HEAD_DIM))


def _router_kernel(x_ref, g_ref, sh_ref, sc_ref, whi_ref, wlo_ref, rb_ref, h_ref, ri_ref, rw_ref, cnt_ref,
                   run_ref, *, n_groups, epg):
    @pl.when((pl.program_id(0) == 0) & (pl.program_id(1) == 0))
    def _():
        run_ref[...] = jnp.zeros_like(run_ref)

    x = x_ref[...]
    y = x * lax.rsqrt(jnp.mean(x * x, axis=-1, keepdims=True) + EPS) * g_ref[...]
    h = y * (1.0 + sc_ref[...]) + sh_ref[...]
    h_ref[...] = h
    h_hi = h.astype(BF16)
    h_lo = (h - h_hi.astype(F32)).astype(BF16)
    whi = whi_ref[...]
    logits = _dot(h_hi, whi) + _dot(h_hi, wlo_ref[...]) + _dot(h_lo, whi) + rb_ref[...]
    ts = logits.shape[0]
    lane = lax.broadcasted_iota(jnp.int32, (ts, LANES), 1)
    lane_f = lane.astype(F32)
    gmask = lane < n_groups
    lg = jnp.where(gmask, logits, NEG_INF)
    gmax = jnp.max(lg, axis=-1, keepdims=True)
    grp = jnp.min(jnp.where(lg == gmax, lane_f, float(LANES)), axis=-1, keepdims=True)
    zg = jnp.sum(jnp.where(gmask, jnp.exp(jnp.where(gmask, logits - gmax, 0.0)), 0.0), axis=-1, keepdims=True)
    p_grp = 1.0 / zg
    lo = float(n_groups) + grp * float(epg)
    emask = (lane_f >= lo) & (lane_f < lo + float(epg))
    le = jnp.where(emask, logits, NEG_INF)
    m1 = jnp.max(le, axis=-1, keepdims=True)
    i1 = jnp.min(jnp.where(le == m1, lane_f, float(LANES)), axis=-1, keepdims=True)
    le2 = jnp.where(lane_f == i1, NEG_INF, le)
    m2 = jnp.max(le2, axis=-1, keepdims=True)
    i2 = jnp.min(jnp.where(le2 == m2, lane_f, float(LANES)), axis=-1, keepdims=True)
    e2 = jnp.exp(m2 - m1)
    w1 = p_grp / (1.0 + e2)
    w2 = p_grp * e2 / (1.0 + e2)
    hot1 = lane_f == i1
    hot2 = lane_f == i2
    onehot = jnp.where(hot1 | hot2, 1.0, 0.0)
    ti = lax.broadcasted_iota(jnp.int32, (ts, ts), 0)
    tj = lax.broadcasted_iota(jnp.int32, (ts, ts), 1)
    tri = jnp.where(ti > tj, 1.0, 0.0).astype(BF16)
    before = _dot(tri, onehot.astype(BF16)) + run_ref[0:1, :]
    r1 = jnp.sum(jnp.where(hot1, before, 0.0), axis=-1, keepdims=True)
    r2 = jnp.sum(jnp.where(hot2, before, 0.0), axis=-1, keepdims=True)
    run_ref[...] = run_ref[...] + jnp.sum(onehot, axis=0, keepdims=True)
    cnt_ref[...] = run_ref[...]
    ng = float(n_groups)
    ri = jnp.where(lane == 0, i1 - ng, jnp.where(lane == 1, i2 - ng, jnp.where(lane == 2, r1, jnp.where(
        lane == 3, r2, 0.0))))
    ri_ref[...] = ri.astype(jnp.int32)
    rw_ref[...] = jnp.where(lane == 0, w1, jnp.where(lane == 1, w2, 0.0))


def _router(x, g, mod4, shift_chunk, scale_chunk, w_hi, w_lo, rbias, ts, ncb, g_tile0):
    bsz, s, d = x.shape
    ctx_row = bsz
    nt = s // ts
    t_moe = bsz * nt * ts
    tok = lambda b, j: (b * nt + j, 0)
    return pl.pallas_call(
        functools.partial(_router_kernel, n_groups=N_GROUPS, epg=EXPERTS_PER_GROUP),
        out_shape=(jax.ShapeDtypeStruct((t_moe, d), F32), jax.ShapeDtypeStruct((t_moe, LANES), jnp.int32),
                   jax.ShapeDtypeStruct((t_moe, LANES), F32), jax.ShapeDtypeStruct((8, LANES), F32)),
        grid=(bsz, nt),
        in_specs=[
            pl.BlockSpec((None, ts, d), lambda b, j: (b, j, 0)),
            pl.BlockSpec((1, d), lambda b, j: (0, 0)),
            pl.BlockSpec((None, 1, d), lambda b, j: (_mod_row(b, j + g_tile0, ncb, ctx_row), 0, shift_chunk)),
            pl.BlockSpec((None, 1, d), lambda b, j: (_mod_row(b, j + g_tile0, ncb, ctx_row), 0, scale_chunk)),
            pl.BlockSpec((d, LANES), lambda b, j: (0, 0)),
            pl.BlockSpec((d, LANES), lambda b, j: (0, 0)),
            pl.BlockSpec((1, LANES), lambda b, j: (0, 0)),
        ],
        out_specs=(pl.BlockSpec((ts, d), tok), pl.BlockSpec((ts, LANES), tok), pl.BlockSpec((ts, LANES), tok),
                   pl.BlockSpec((8, LANES), lambda b, j: (0, 0))),
        scratch_shapes=[pltpu.VMEM((8, LANES), F32)],
        compiler_params=_cp(("arbitrary", "arbitrary")),
        name="moe_router",
    )(x, g.reshape(1, d), mod4, mod4, w_hi, w_lo, rbias)


def _slotmap_kernel(pos_ref, src_ref, *, n_assign, n_slots):
    def init(i, c):
        src_ref[i] = 0
        return c

    lax.fori_loop(0, n_slots, init, 0, unroll=16)

    def body(a, c):
        src_ref[pos_ref[a]] = lax.shift_right_logical(a, 1)
        return c

    lax.fori_loop(0, n_assign, body, 0, unroll=16)


def _slotmap(pos_flat, n_slots):
    n_assign = pos_flat.shape[0]
    return pl.pallas_call(
        functools.partial(_slotmap_kernel, n_assign=n_assign, n_slots=n_slots),
        out_shape=jax.ShapeDtypeStruct((n_slots,), jnp.int32),
        in_specs=[pl.BlockSpec(memory_space=pltpu.SMEM)],
        out_specs=pl.BlockSpec(memory_space=pltpu.SMEM),
        name="moe_slotmap",
    )(pos_flat)


def _ffn_kernel(te_ref, nused_ref, src_ref, h_hbm, wg_ref, wu_ref, wd_ref, o_ref, xbuf, sem, *, tm):
    i = pl.program_id(0)
    n_used = nused_ref[0]
    slot = lax.rem(i, 2)

    def row_copy(r, tok, sl):
        return pltpu.make_async_copy(h_hbm.at[pl.ds(tok, 1)], xbuf.at[sl, pl.ds(r, 1)], sem.at[sl])

    def gather(tile, sl):
        def issue(r, c):
            row_copy(r, src_ref[tile * tm + r], sl).start()
            return c

        lax.fori_loop(0, tm, issue, 0, unroll=8)

    @pl.when((i == 0) & (n_used > 0))
    def _():
        gather(0, 0)

    @pl.when(i + 1 < n_used)
    def _():
        gather(i + 1, 1 - slot)

    @pl.when(i < n_used)
    def _():
        def wait(r, c):
            row_copy(r, 0, slot).wait()
            return c

        lax.fori_loop(0, tm, wait, 0, unroll=8)
        x = xbuf[slot].astype(BF16)
        hg = _dot(x, wg_ref[...])
        hu = _dot(x, wu_ref[...])
        o_ref[...] = _dot((_silu(hg) * hu).astype(BF16), wd_ref[...])

    @pl.when(i >= nused_ref[0])
    def _():
        o_ref[...] = jnp.zeros_like(o_ref)


def _expert_ffn(h2, w_gate, w_up, w_down, tile_expert, n_used, src, n_tiles):
    t_moe, d = h2.shape
    n_exp, _, ff = w_gate.shape
    tm = MOE_TILE
    grid_spec = pltpu.PrefetchScalarGridSpec(
        num_scalar_prefetch=3,
        grid=(n_tiles,),
        in_specs=[
            pl.BlockSpec(memory_space=pl.ANY),
            pl.BlockSpec((None, d, ff), lambda i, te, nu, sr: (te[i], 0, 0)),
            pl.BlockSpec((None, d, ff), lambda i, te, nu, sr: (te[i], 0, 0)),
            pl.BlockSpec((None, ff, d), lambda i, te, nu, sr: (te[i], 0, 0)),
        ],
        out_specs=pl.BlockSpec((tm, d), lambda i, te, nu, sr: (i, 0)),
        scratch_shapes=[pltpu.VMEM((2, tm, d), F32), pltpu.SemaphoreType.DMA((2,))],
    )
    return pl.pallas_call(
        functools.partial(_ffn_kernel, tm=tm),
        out_shape=jax.ShapeDtypeStruct((n_tiles * tm, d), F32),
        grid_spec=grid_spec,
        compiler_params=_cp(("arbitrary",)),
        name="moe_ffn",
    )(tile_expert, n_used, src, h2, w_gate, w_up, w_down)


def _combine_kernel(pos_ref, x_ref, gate_ref, rw_ref, fg_ref, ys_hbm, o_ref, buf, sem, *, ts, nt, n_tiles, final):
    t = pl.program_id(0) * nt + pl.program_id(1)
    slot = lax.rem(t, 2)

    def row_copy(r, kk, src_row, sl):
        return pltpu.make_async_copy(ys_hbm.at[pl.ds(src_row, 1)], buf.at[sl, kk, pl.ds(r, 1)], sem.at[sl])

    def gather(tile, sl):
        def issue(r, c):
            for kk in range(2):
                row_copy(r, kk, pos_ref[(tile * ts + r) * 2 + kk], sl).start()
            return c

        lax.fori_loop(0, ts, issue, 0, unroll=4)

    @pl.when(t == 0)
    def _():
        gather(0, 0)

    @pl.when(t + 1 < n_tiles)
    def _():
        gather(t + 1, 1 - slot)

    def wait(r, c):
        for kk in range(2):
            row_copy(r, kk, 0, slot).wait()
        return c

    lax.fori_loop(0, ts, wait, 0, unroll=4)
    rw = rw_ref[...]
    y = rw[:, 0:1] * buf[slot, 0] + rw[:, 1:2] * buf[slot, 1]
    xn = x_ref[...] + gate_ref[...] * y
    if final:
        xn = xn * lax.rsqrt(jnp.mean(xn * xn, axis=-1, keepdims=True) + EPS) * fg_ref[...]
    o_ref[...] = xn


def _combine(x, mod4, gate_chunk, rw, ys, pos_flat, final_g, ts, ncb, g_tile0, final):
    bsz, s, d = x.shape
    ctx_row = bsz
    nt = s // ts
    grid_spec = pltpu.PrefetchScalarGridSpec(
        num_scalar_prefetch=1,
        grid=(bsz, nt),
        in_specs=[
            pl.BlockSpec((None, ts, d), lambda b, j, p: (b, j, 0)),
            pl.BlockSpec((None, 1, d), lambda b, j, p: (_mod_row(b, j + g_tile0, ncb, ctx_row), 0, gate_chunk)),
            pl.BlockSpec((ts, LANES), lambda b, j, p: (b * nt + j, 0)),
            pl.BlockSpec((1, d), lambda b, j, p: (0, 0)),
            pl.BlockSpec(memory_space=pl.ANY),
        ],
        out_specs=pl.BlockSpec((None, ts, d), lambda b, j, p: (b, j, 0)),
        scratch_shapes=[pltpu.VMEM((2, 2, ts, d), F32), pltpu.SemaphoreType.DMA((2,))],
    )
    return pl.pallas_call(
        functools.partial(_combine_kernel, ts=ts, nt=nt, n_tiles=bsz * nt, final=final),
        out_shape=jax.ShapeDtypeStruct((bsz, s, d), F32),
        grid_spec=grid_spec,
        compiler_params=_cp(("arbitrary", "arbitrary")),
        name="moe_combine",
    )(pos_flat, x, mod4, rw, final_g.reshape(1, d), ys)


def _in_layout(gdn_w, att_w, kv_w, hg_w, tn):
    lay = {"gdn_w": gdn_w, "att_w": att_w, "kv_w": kv_w, "hg_w": hg_w}
    cur = 0
    for name, w in (("gq", gdn_w), ("gk", gdn_w), ("gv", gdn_w), ("gz", gdn_w), ("aq", att_w), ("ak", kv_w),
                    ("av", kv_w), ("ab", LANES), ("hq", hg_w), ("hff", hg_w), ("hfb", hg_w), ("hi", hg_w),
                    ("hg", hg_w)):
        cur = _round_up(cur, w)
        lay[name] = cur
        cur += w
    lay["n"] = _round_up(cur, tn)
    return lay


def _prep_w_in(w_in_l, lay, n_gdn_heads):
    d = w_in_l.shape[0]
    gdn_w, att_w, kv_w, hg_w = lay["gdn_w"], lay["att_w"], lay["kv_w"], lay["hg_w"]
    sizes = (gdn_w, gdn_w, gdn_w, gdn_w, 4 * n_gdn_heads, att_w, kv_w, kv_w, hg_w, hg_w, hg_w, hg_w, hg_w)
    names = ("gq", "gk", "gv", "gz", "ab", "aq", "ak", "av", "hq", "hff", "hfb", "hi", "hg")
    offs = [0]
    for sz in sizes:
        offs.append(offs[-1] + sz)
    pieces = sorted((lay[nm], w_in_l[:, offs[i]:offs[i + 1]]) for i, nm in enumerate(names))
    out, cur = [], 0
    for start, piece in pieces:
        if start > cur:
            out.append(jnp.zeros((d, start - cur), w_in_l.dtype))
        out.append(piece)
        cur = start + piece.shape[1]
    if lay["n"] > cur:
        out.append(jnp.zeros((d, lay["n"] - cur), w_in_l.dtype))
    return jnp.concatenate(out, axis=1).astype(BF16)


def _rope_tables(n_ctx, n_lat):
    rows = n_lat // GRID_W
    row = jnp.repeat(jnp.arange(rows, dtype=F32), GRID_W)
    col = jnp.tile(jnp.arange(GRID_W, dtype=F32), rows)
    half = HEAD_DIM // 2
    inv_freq = ROPE_THETA ** (-jnp.arange(0, half, 2, dtype=F32) / half)
    ang_r = row[:, None] * inv_freq[None, :]
    ang_c = col[:, None] * inv_freq[None, :]
    ang = jnp.concatenate([ang_r, ang_r, ang_c, ang_c], axis=-1)
    cos, sin = jnp.cos(ang), jnp.sin(ang)
    lane = jnp.arange(HEAD_DIM)
    low = (lane % half) < half // 2
    sa = jnp.where(low, -sin, 0.0)
    sb = jnp.where(low, 0.0, sin)
    pad1 = jnp.ones((n_ctx, HEAD_DIM), F32)
    pad0 = jnp.zeros((n_ctx, HEAD_DIM), F32)
    return (jnp.concatenate([pad1, cos]), jnp.concatenate([pad0, sa]), jnp.concatenate([pad0, sb]))


def _largest_divisor(n, cap):
    for c in range(min(n, cap), 0, -1):
        if n % c == 0:
            return c
    return 1


def kernel(x, c, ctx, c_ctx, ada_w, ada_b, norm1_g, norm2_g, w_in, gdn_conv_w, gdn_a_log, gdn_dt_bias,
           gdn_norm_g, attn_q_norm_g, attn_k_norm_g, hgrn_lb_logits, hgrn_norm_g, w_out,
           router_group_w, router_group_b, router_expert_w, router_expert_b,
           moe_w_gate, moe_w_up, moe_w_down, final_norm_g):
    bsz, n_lat, d = x.shape
    n_ctx = ctx.shape[1]
    s = n_ctx + n_lat
    depth = ada_w.shape[0]
    n_gdn = gdn_a_log.shape[-1]
    gdn_w = n_gdn * HEAD_DIM
    hg_w = hgrn_lb_logits.shape[-1]
    att_w = w_out.shape[1] - gdn_w - hg_w
    kv_w = (w_in.shape[-1] - 4 * gdn_w - 4 * n_gdn - att_w - 5 * hg_w) // 2
    group = att_w // kv_w
    n_exp = moe_w_gate.shape[1]
    assert n_exp == N_GROUPS * EXPERTS_PER_GROUP and N_GROUPS + n_exp <= LANES
    assert n_ctx % CHUNK == 0 and n_lat % CHUNK == 0 and n_lat % GRID_W == 0
    assert 4 * n_gdn <= LANES

    ts = math.gcd(math.gcd(n_ctx, n_lat), 256)
    ncb = n_ctx // ts
    hb_g = _largest_divisor(n_gdn, 12)
    hb_h = _largest_divisor(hg_w // HEAD_DIM, 4)
    tn_in = 2048
    lay = _in_layout(gdn_w, att_w, kv_w, hg_w, tn_in)
    tm_in = _largest_divisor(bsz * s // ts, 2) * ts

    rows = _round_up(bsz + 1, 8)
    cc = jnp.concatenate([c, c_ctx[None, :], jnp.zeros((rows - bsz - 1, d), F32)], axis=0)
    mod = _ada_modulation(cc, ada_w, ada_b)
    tables = _rope_tables(n_ctx, n_lat)

    xc = jnp.concatenate([ctx, x], axis=1)
    out = None
    for l in range(depth):
        last = l == depth - 1
        row_tile0 = ncb if last else 0
        mod4 = mod[l].reshape(rows, 1, ADA_CHUNKS * d)
        w_in_p = _prep_w_in(w_in[l], lay, n_gdn)

        h = _norm_mod(xc, norm1_g[l], mod4, 0, 1, ts, ncb)
        proj = _matmul(h.reshape(bsz * s, d), w_in_p, tm_in, tn_in).reshape(bsz, s, lay["n"])
        qkv = _gdn_prep(proj, gdn_conv_w[l], lay, n_ctx, hb_g)
        gof, gob = _gdn_scan(qkv, proj, gdn_a_log[l], gdn_dt_bias[l], lay, n_ctx, hb_g)
        hof, hob = _hgrn_scan(proj, hgrn_lb_logits, lay, n_ctx, hb_h, l)
        qn, kn, vn = _att_prep(proj, tables, attn_q_norm_g[l], attn_k_norm_g[l], lay, ts)
        att = _attention(qn, kn, vn, ts, row_tile0, n_ctx, group)
        y = _finalize(gof, gob, att, hof, hob, proj, gdn_norm_g[l], hgrn_norm_g[l], lay, ts, row_tile0)
        xc = _out_proj(y, w_out[l].astype(BF16), xc, mod4, 2, ts, ncb, row_tile0)

        rw_cat = jnp.concatenate([router_group_w[l], router_expert_w[l],
                                  jnp.zeros((d, LANES - N_GROUPS - n_exp), F32)], axis=1)
        rw_hi = rw_cat.astype(BF16)
        rw_lo = (rw_cat - rw_hi.astype(F32)).astype(BF16)
        rbias = jnp.concatenate([router_group_b[l], router_expert_b[l],
                                 jnp.zeros((LANES - N_GROUPS - n_exp,), F32)]).reshape(1, LANES)
        h2, ri, rwts, cnt = _router(xc, norm2_g[l], mod4, 3, 4, rw_hi, rw_lo, rbias, ts, ncb, row_tile0)
        t_moe = h2.shape[0]
        counts = cnt[0, N_GROUPS:N_GROUPS + n_exp].astype(jnp.int32)
        padded = (counts + MOE_TILE - 1) // MOE_TILE * MOE_TILE
        ends = jnp.cumsum(padded)
        offs = ends - padded
        n_tiles = 2 * t_moe // MOE_TILE + n_exp
        tile_start = jnp.arange(n_tiles, dtype=jnp.int32) * MOE_TILE
        tile_expert = jnp.minimum(jnp.sum((tile_start[:, None] >= ends[None, :]).astype(jnp.int32), axis=1),
                                  n_exp - 1).astype(jnp.int32)
        n_used = (ends[-1:] // MOE_TILE).astype(jnp.int32)
        pos = (offs[ri[:, 0:2]] + ri[:, 2:4]).reshape(-1).astype(jnp.int32)
        src = _slotmap(pos, n_tiles * MOE_TILE)
        ys = _expert_ffn(h2, moe_w_gate[l].astype(BF16), moe_w_up[l].astype(BF16), moe_w_down[l].astype(BF16),
                         tile_expert, n_used, src, n_tiles)
        res = _combine(xc, mod4, 5, rwts, ys, pos, final_norm_g, ts, ncb, row_tile0, last)
        if last:
            out = res
        else:
            xc = res
    return out
```
